```python
import math
import jax, jax.numpy as jnp
from jax import lax
import numpy as np

D_MODEL = 2048
BATCH = 1
SEQ = 16384
DEPTH = 4

N_META = 16
MIX_WIDTH = D_MODEL
CONV_HEADS = 8
CONV_CH = MIX_WIDTH // 2
CONV_K = 31
POOL_WINDOWS = (2, 4, 8, 16)
POOL_GROUPS = len(POOL_WINDOWS)
POOL_CH = MIX_WIDTH - CONV_CH
POOL_GROUP_CH = POOL_CH // POOL_GROUPS
IN_COLS = 2 * CONV_CH + POOL_CH
N_EXPERT_GROUPS = 4
EXPERTS_PER_GROUP = 8
N_EXPERTS = N_EXPERT_GROUPS * EXPERTS_PER_GROUP
TOP_K = 2
EXPERT_HIDDEN = D_MODEL // 4
EXPERT_BLOCK = 128
DEEPNORM_ALPHA = (2.0 * DEPTH) ** 0.25
DEEPNORM_BETA = (8.0 * DEPTH) ** -0.25
LN_EPS = 1e-5

kernel_name = "hymba_conformer_pool_hmoe_deepnorm"


def layer_norm(x, g, b):
    xf = x.astype(jnp.float32)
    mu = jnp.mean(xf, axis=-1, keepdims=True)
    xc = xf - mu
    var = jnp.mean(xc * xc, axis=-1, keepdims=True)
    y = xc * lax.rsqrt(var + LN_EPS)
    return (y * g.astype(jnp.float32) + b.astype(jnp.float32)).astype(x.dtype)


def multiscale_pool(u, pool_w, pool_b, pool_scale):
    B, L, C = u.shape
    csp = jnp.pad(lax.cumsum(u.astype(jnp.float32), axis=1), ((0, 0), (1, 0), (0, 0)))
    t = jnp.arange(L)
    means = []
    for g, w in enumerate(POOL_WINDOWS):
        sl = slice(g * POOL_GROUP_CH, (g + 1) * POOL_GROUP_CH)
        lo = jnp.maximum(t + 1 - w, 0)
        cnt = (t + 1 - lo).astype(jnp.float32)[None, :, None]
        means.append((csp[:, 1:, sl] - csp[:, lo, sl]) / cnt)
    pooled = jnp.stack(means, axis=2)
    ug = u.reshape(B, L, POOL_GROUPS, POOL_GROUP_CH)
    d = (pooled - ug.astype(jnp.float32)).astype(u.dtype)
    y = jnp.einsum('blgc,gce->blge', d, pool_w) + pool_b
    return y.reshape(B, L, C) * pool_scale


def hybrid_mixer(h, w_in, b_in, conv_w, conv_b, conv_ln_g, conv_ln_b, pool_w, pool_b, pool_scale, w_out, b_out):
    proj = jnp.einsum('bld,dc->blc', h, w_in) + b_in
    a = proj[..., :CONV_CH]
    gate = proj[..., CONV_CH:2 * CONV_CH]
    u = proj[..., 2 * CONV_CH:]
    v = a * jax.nn.sigmoid(gate)
    v = lax.conv_general_dilated(v, conv_w, window_strides=(1,), padding=[(CONV_K - 1, 0)],
                                 dimension_numbers=('NWC', 'WIO', 'NWC'),
                                 feature_group_count=CONV_CH) + conv_b
    v = jax.nn.silu(layer_norm(v, conv_ln_g, conv_ln_b))
    p = multiscale_pool(u, pool_w, pool_b, pool_scale)
    y = jnp.concatenate([v, p], axis=-1)
    return jnp.einsum('blc,cd->bld', y, w_out) + b_out


def hierarchical_moe(h, rg_w, rg_b, re_w, re_b, w_gate, w_up, w_down):
    B, L, D = h.shape
    T = B * L
    xf = h.reshape(T, D)
    g_prob = jax.nn.softmax((xf @ rg_w + rg_b).astype(jnp.float32), axis=-1)
    g_idx = jnp.argmax(g_prob, axis=-1)
    g_p = jnp.take_along_axis(g_prob, g_idx[:, None], axis=1)[:, 0]
    e_logits = (xf @ re_w + re_b).astype(jnp.float32).reshape(T, N_EXPERT_GROUPS, EXPERTS_PER_GROUP)
    e_logits = jnp.take_along_axis(e_logits, g_idx[:, None, None], axis=1)[:, 0]
    e_prob = jax.nn.softmax(e_logits, axis=-1)
    top_p, top_i = lax.top_k(e_prob, TOP_K)
    top_p = top_p / jnp.sum(top_p, axis=-1, keepdims=True)
    gates = g_p[:, None] * top_p
    expert = (g_idx[:, None] * EXPERTS_PER_GROUP + top_i).astype(jnp.int32)
    n_assign = T * TOP_K
    e_flat = expert.reshape(-1)
    gate_flat = gates.reshape(-1)
    tok_flat = jnp.repeat(jnp.arange(T, dtype=jnp.int32), TOP_K)
    counts = jnp.bincount(e_flat, length=N_EXPERTS)
    padded = (counts + EXPERT_BLOCK - 1) // EXPERT_BLOCK * EXPERT_BLOCK
    start = jnp.cumsum(counts) - counts
    pend = jnp.cumsum(padded)
    pstart = pend - padded
    order = jnp.argsort(e_flat, stable=True)
    se = e_flat[order]
    dest = pstart[se] + jnp.arange(n_assign) - start[se]
    n_blocks = -(-(n_assign + N_EXPERTS * (EXPERT_BLOCK - 1)) // EXPERT_BLOCK)
    n_slots = n_blocks * EXPERT_BLOCK
    slot_tok = jnp.full((n_slots,), T, jnp.int32).at[dest].set(tok_flat[order])
    slot_gate = jnp.zeros((n_slots,), jnp.float32).at[dest].set(gate_flat[order])
    block_expert = jnp.minimum(jnp.searchsorted(pend, jnp.arange(n_blocks) * EXPERT_BLOCK, side='right'),
                               N_EXPERTS - 1)
    x_pad = jnp.concatenate([xf, jnp.zeros((1, D), xf.dtype)], axis=0)
    x_blocks = x_pad[slot_tok].reshape(n_blocks, EXPERT_BLOCK, D)

    def expert_block(args):
        xb, e = args
        hb = jax.nn.silu(xb @ w_gate[e]) * (xb @ w_up[e])
        return hb @ w_down[e]

    y = lax.map(expert_block, (x_blocks, block_expert)).reshape(n_slots, D)
    y = y * slot_gate[:, None].astype(y.dtype)
    out = jnp.zeros((T + 1, D), y.dtype).at[slot_tok].add(y)[:T]
    return out.reshape(B, L, D)


def setup_inputs(seed: int = 0) -> dict:
    key = jax.random.key(seed)
    ks = jax.random.split(key, 32)
    n = lambda k, s: jax.random.normal(k, s, jnp.float32)
    D = D_MODEL
    return {
        "x": n(ks[0], (BATCH, SEQ, D)),
        "meta_tokens": n(ks[1], (N_META, D)),
        "ln_in_g": 1.0 + 0.02 * n(ks[2], (D,)),
        "ln_in_b": 0.02 * n(ks[3], (D,)),
        "w_in": n(ks[4], (DEPTH, D, IN_COLS)) * D ** -0.5,
        "b_in": 0.02 * n(ks[5], (DEPTH, IN_COLS)),
        "conv_w": n(ks[6], (DEPTH, CONV_K, 1, CONV_CH)) * CONV_K ** -0.5,
        "conv_b": 0.02 * n(ks[7], (DEPTH, CONV_CH)),
        "conv_ln_g": 1.0 + 0.02 * n(ks[8], (DEPTH, CONV_CH)),
        "conv_ln_b": 0.02 * n(ks[9], (DEPTH, CONV_CH)),
        "pool_w": n(ks[10], (DEPTH, POOL_GROUPS, POOL_GROUP_CH, POOL_GROUP_CH)) * POOL_GROUP_CH ** -0.5,
        "pool_b": 0.02 * n(ks[11], (DEPTH, POOL_GROUPS, POOL_GROUP_CH)),
        "pool_scale": 1.0 + 0.02 * n(ks[12], (DEPTH, POOL_CH)),
        "w_out": n(ks[13], (DEPTH, MIX_WIDTH, D)) * (MIX_WIDTH ** -0.5 * DEEPNORM_BETA),
        "b_out": 0.02 * n(ks[14], (DEPTH, D)),
        "ln_mix_g": 1.0 + 0.02 * n(ks[15], (DEPTH, D)),
        "ln_mix_b": 0.02 * n(ks[16], (DEPTH, D)),
        "router_group_w": n(ks[17], (DEPTH, D, N_EXPERT_GROUPS)) * D ** -0.5,
        "router_group_b": 0.01 * n(ks[18], (DEPTH, N_EXPERT_GROUPS)),
        "router_expert_w": n(ks[19], (DEPTH, D, N_EXPERTS)) * D ** -0.5,
        "router_expert_b": 0.01 * n(ks[20], (DEPTH, N_EXPERTS)),
        "w_gate": n(ks[21], (DEPTH, N_EXPERTS, D, EXPERT_HIDDEN)) * D ** -0.5,
        "w_up": n(ks[22], (DEPTH, N_EXPERTS, D, EXPERT_HIDDEN)) * D ** -0.5,
        "w_down": n(ks[23], (DEPTH, N_EXPERTS, EXPERT_HIDDEN, D)) * (EXPERT_HIDDEN ** -0.5 * DEEPNORM_BETA),
        "ln_ffn_g": 1.0 + 0.02 * n(ks[24], (DEPTH, D)),
        "ln_ffn_b": 0.02 * n(ks[25], (DEPTH, D)),
    }


def reference(x, meta_tokens, ln_in_g, ln_in_b, w_in, b_in, conv_w, conv_b, conv_ln_g, conv_ln_b,
              pool_w, pool_b, pool_scale, w_out, b_out, ln_mix_g, ln_mix_b,
              router_group_w, router_group_b, router_expert_w, router_expert_b,
              w_gate, w_up, w_down, ln_ffn_g, ln_ffn_b):
    B = x.shape[0]
    meta = jnp.broadcast_to(meta_tokens.astype(x.dtype)[None], (B, N_META, x.shape[-1]))
    h = jnp.concatenate([meta, x], axis=1)
    h = layer_norm(h, ln_in_g, ln_in_b)
    for i in range(DEPTH):
        mix = hybrid_mixer(h, w_in[i], b_in[i], conv_w[i], conv_b[i], conv_ln_g[i], conv_ln_b[i],
                           pool_w[i], pool_b[i], pool_scale[i], w_out[i], b_out[i])
        h = layer_norm(DEEPNORM_ALPHA * h + mix, ln_mix_g[i], ln_mix_b[i])
        ffn = hierarchical_moe(h, router_group_w[i], router_group_b[i], router_expert_w[i], router_expert_b[i],
                               w_gate[i], w_up[i], w_down[i])
        h = layer_norm(DEEPNORM_ALPHA * h + ffn, ln_ffn_g[i], ln_ffn_b[i])
    return h[:, N_META:]
```

```python
import functools

import jax
import jax.numpy as jnp
from jax import lax
from jax.experimental import pallas as pl
from jax.experimental.pallas import tpu as pltpu

N_META = 16
CONV_K = 31
POOL_WINDOWS = (2, 4, 8, 16)
N_EXPERT_GROUPS = 4
EXPERTS_PER_GROUP = 8
N_EXPERTS = N_EXPERT_GROUPS * EXPERTS_PER_GROUP
LN_EPS = 1e-5

LANES = 128
SUBLANES = 8
CONV_HALO = 32
POOL_HALO = 16
ROW_TILE_CANDIDATES = (400, 656, 328, 200, 80, 16, 8)
CONV_ROW_BLOCK = 40
EXPERT_BLOCK = 256
EXPERT_LANE0 = N_EXPERT_GROUPS
VMEM_LIMIT = 56 * 1024 * 1024

MXU_DTYPE = jnp.bfloat16


def _row_tile(n_rows):
    for t in ROW_TILE_CANDIDATES:
        if n_rows % t == 0:
            return t
    raise ValueError(f"no row tile divides {n_rows}")


def _conv_row_block(tm):
    for rb in (CONV_ROW_BLOCK, 16, 8):
        if tm % rb == 0:
            return rb
    raise ValueError(f"no conv row block divides {tm}")


def _params(n_axes=1):
    return pltpu.CompilerParams(dimension_semantics=("arbitrary",) * n_axes, vmem_limit_bytes=VMEM_LIMIT)


def _const_spec(shape):
    return pl.BlockSpec(shape, lambda i: (0,) * len(shape), pipeline_mode=pl.Buffered(1))


def _layer_spec(shape, layer):
    return pl.BlockSpec((1,) + tuple(shape[1:]), lambda i: (layer, 0, 0), pipeline_mode=pl.Buffered(1))


def _layer_norm(x, g, b):
    mu = jnp.mean(x, axis=-1, keepdims=True)
    xc = x - mu
    var = jnp.mean(xc * xc, axis=-1, keepdims=True)
    return xc * lax.rsqrt(var + LN_EPS) * g + b


def _dot(a, b):
    return jnp.dot(a.astype(MXU_DTYPE), b.astype(MXU_DTYPE), preferred_element_type=jnp.float32)


def _ln_kernel(x_ref, g_ref, b_ref, o_ref):
    o_ref[...] = _layer_norm(x_ref[...], g_ref[...], b_ref[...])


def _input_ln(h, g, b, tm):
    n, d = h.shape
    return pl.pallas_call(
        _ln_kernel,
        grid=(n // tm,),
        in_specs=[pl.BlockSpec((tm, d), lambda i: (i, 0)), _const_spec((1, d)), _const_spec((1, d))],
        out_specs=pl.BlockSpec((tm, d), lambda i: (i, 0)),
        out_shape=jax.ShapeDtypeStruct((n, d), jnp.float32),
        compiler_params=_params(),
        name="input_ln",
    )(h, g.reshape(1, d), b.reshape(1, d))


def _mix_front_kernel(h_ref, w_in_ref, b_in_ref, conv_w_ref, conv_b_ref, cg_ref, cb_ref,
                      pool_w_ref, pool_b_ref, pool_scale_ref, y_ref,
                      vbuf, ubuf, shifted, conv_out, *, tm, c_conv, c_pool):
    i = pl.program_id(0)

    @pl.when(i == 0)
    def _():
        vbuf[0:CONV_HALO, :] = jnp.zeros((CONV_HALO, c_conv), jnp.float32)
        ubuf[0:POOL_HALO, :] = jnp.zeros((POOL_HALO, c_pool), jnp.float32)

    xb = h_ref[...].astype(MXU_DTYPE)
    a = _dot(xb, w_in_ref[0, :, 0:c_conv]) + b_in_ref[:, 0:c_conv]
    gate = _dot(xb, w_in_ref[0, :, c_conv:2 * c_conv]) + b_in_ref[:, c_conv:2 * c_conv]
    vbuf[CONV_HALO:CONV_HALO + tm, :] = a * jax.nn.sigmoid(gate)
    ubuf[POOL_HALO:POOL_HALO + tm, :] = _dot(xb, w_in_ref[0, :, 2 * c_conv:]) + b_in_ref[:, 2 * c_conv:]

    rb = _conv_row_block(tm)
    n_sh = tm + CONV_HALO - SUBLANES
    for c in range(c_conv // LANES):
        cs = slice(c * LANES, (c + 1) * LANES)
        for r in range(1, SUBLANES):
            shifted[r - 1, :, :] = vbuf[r:r + n_sh, cs]

        def conv_rows(j, carry):
            r0 = pl.multiple_of(j * rb, SUBLANES)
            acc = jnp.zeros((rb, LANES), jnp.float32)
            for k in range(CONV_K):
                q, r = divmod(CONV_HALO - (CONV_K - 1) + k, SUBLANES)
                start = pl.multiple_of(r0 + q * SUBLANES, SUBLANES)
                if r == 0:
                    src = vbuf[pl.ds(start, rb), cs]
                else:
                    src = shifted[r - 1, pl.ds(start, rb), :]
                acc = acc + conv_w_ref[k:k + 1, cs] * src
            conv_out[pl.ds(r0, rb), cs] = acc + conv_b_ref[:, cs]
            return carry

        lax.fori_loop(0, tm // rb, conv_rows, 0)

    v = _layer_norm(conv_out[...], cg_ref[...], cb_ref[...])
    y_ref[:, 0:c_conv] = (v * jax.nn.sigmoid(v)).astype(y_ref.dtype)

    t_glob = i * tm + lax.broadcasted_iota(jnp.int32, (tm, 1), 0)
    cg_ch = c_pool // len(POOL_WINDOWS)
    for g, w in enumerate(POOL_WINDOWS):
        gs = slice(g * cg_ch, (g + 1) * cg_ch)
        u = ubuf[POOL_HALO:POOL_HALO + tm, gs]
        s = u
        for j in range(1, w):
            s = s + ubuf[POOL_HALO - j:POOL_HALO - j + tm, gs]
        cnt = jnp.minimum(t_glob + 1, w).astype(jnp.float32)
        d = s / cnt - u
        p = _dot(d, pool_w_ref[g]) + pool_b_ref[:, gs]
        y_ref[:, c_conv + g * cg_ch:c_conv + (g + 1) * cg_ch] = (p * pool_scale_ref[:, gs]).astype(y_ref.dtype)

    vbuf[0:CONV_HALO, :] = vbuf[tm:tm + CONV_HALO, :]
    ubuf[0:POOL_HALO, :] = ubuf[tm:tm + POOL_HALO, :]


def _mix_front(h, w_in, layer, b_in, conv_w, conv_b, cg, cb, pool_w, pool_b, pool_scale, tm):
    n, d = h.shape
    c_conv = conv_w.shape[-1]
    c_pool = pool_scale.shape[-1]
    kern = functools.partial(_mix_front_kernel, tm=tm, c_conv=c_conv, c_pool=c_pool)
    return pl.pallas_call(
        kern,
        grid=(n // tm,),
        in_specs=[
            pl.BlockSpec((tm, d), lambda i: (i, 0)),
            _layer_spec(w_in.shape, layer),
            _const_spec((1, b_in.shape[-1])),
            _const_spec((CONV_K, c_conv)),
            _const_spec((1, c_conv)),
            _const_spec((1, c_conv)),
            _const_spec((1, c_conv)),
            _const_spec(pool_w.shape),
            _const_spec((1, c_pool)),
            _const_spec((1, c_pool)),
        ],
        out_specs=pl.BlockSpec((tm, c_conv + c_pool), lambda i: (i, 0)),
        out_shape=jax.ShapeDtypeStruct((n, c_conv + c_pool), MXU_DTYPE),
        scratch_shapes=[
            pltpu.VMEM((CONV_HALO + tm, c_conv), jnp.float32),
            pltpu.VMEM((POOL_HALO + tm, c_pool), jnp.float32),
            pltpu.VMEM((SUBLANES - 1, tm + CONV_HALO - SUBLANES, LANES), jnp.float32),
            pltpu.VMEM((tm, c_conv), jnp.float32),
        ],
        compiler_params=_params(),
        name="mix_front",
    )(h, w_in, b_in.reshape(1, -1), conv_w.reshape(CONV_K, c_conv), conv_b.reshape(1, -1), cg.reshape(1, -1),
      cb.reshape(1, -1), pool_w, pool_b.reshape(1, -1), pool_scale.reshape(1, -1))


def _mix_back_kernel(y_ref, h_ref, w_out_ref, b_out_ref, g_ref, b_ref, wr_ref, br_ref, h1_ref, lg_ref, *, alpha):
    mix = _dot(y_ref[...], w_out_ref[0]) + b_out_ref[...]
    h1 = _layer_norm(alpha * h_ref[...] + mix, g_ref[...], b_ref[...])
    h1_ref[...] = h1
    lg_ref[...] = _dot(h1, wr_ref[...]) + br_ref[...]


def _mix_back(y, h, w_out, layer, b_out, g, b, wr, br, tm, alpha):
    n, d = h.shape
    kern = functools.partial(_mix_back_kernel, alpha=alpha)
    return pl.pallas_call(
        kern,
        grid=(n // tm,),
        in_specs=[
            pl.BlockSpec((tm, y.shape[1]), lambda i: (i, 0)),
            pl.BlockSpec((tm, d), lambda i: (i, 0)),
            _layer_spec(w_out.shape, layer),
            _const_spec((1, d)),
            _const_spec((1, d)),
            _const_spec((1, d)),
            _const_spec(wr.shape),
            _const_spec((1, LANES)),
        ],
        out_specs=[pl.BlockSpec((tm, d), lambda i: (i, 0)), pl.BlockSpec((tm, LANES), lambda i: (i, 0))],
        out_shape=[jax.ShapeDtypeStruct((n, d), jnp.float32), jax.ShapeDtypeStruct((n, LANES), jnp.float32)],
        compiler_params=_params(),
        name="mix_back",
    )(y, h, w_out, b_out.reshape(1, d), g.reshape(1, d), b.reshape(1, d), wr, br)


def _route_kernel(lg_ref, idx_ref, gate_ref, cnt_ref, carry, *, tm):
    i = pl.program_id(0)

    @pl.when(i == 0)
    def _():
        carry[...] = jnp.zeros_like(carry)

    lg = lg_ref[...]
    lane = lax.broadcasted_iota(jnp.int32, (tm, LANES), 1)
    neg = jnp.float32(-jnp.inf)

    def first_lane(mask):
        return jnp.min(jnp.where(mask, lane, LANES), axis=1, keepdims=True)

    gmask = lane < N_EXPERT_GROUPS
    gl = jnp.where(gmask, lg, neg)
    gmax = jnp.max(gl, axis=1, keepdims=True)
    gsum = jnp.sum(jnp.where(gmask, jnp.exp(gl - gmax), 0.0), axis=1, keepdims=True)
    g_p = 1.0 / gsum
    g_idx = first_lane(gl == gmax)

    lo = EXPERT_LANE0 + EXPERTS_PER_GROUP * g_idx
    emask = (lane >= lo) & (lane < lo + EXPERTS_PER_GROUP)
    el = jnp.where(emask, lg, neg)
    emax = jnp.max(el, axis=1, keepdims=True)
    eexp = jnp.where(emask, jnp.exp(el - emax), 0.0)
    eprob = eexp / jnp.sum(eexp, axis=1, keepdims=True)
    p1 = jnp.max(jnp.where(emask, eprob, -1.0), axis=1, keepdims=True)
    i1 = first_lane(emask & (eprob == p1))
    mask2 = emask & (lane != i1)
    p2 = jnp.max(jnp.where(mask2, eprob, -1.0), axis=1, keepdims=True)
    i2 = first_lane(mask2 & (eprob == p2))
    den = p1 + p2
    gate1 = g_p * (p1 / den)
    gate2 = g_p * (p2 / den)

    onehot = ((lane == i1) | (lane == i2)).astype(jnp.float32)
    row = lax.broadcasted_iota(jnp.int32, (tm, tm), 0)
    col = lax.broadcasted_iota(jnp.int32, (tm, tm), 1)
    earlier = (row > col).astype(jnp.bfloat16)
    prefix = jnp.dot(earlier, onehot.astype(jnp.bfloat16), preferred_element_type=jnp.float32) + carry[...]
    r1 = jnp.sum(jnp.where(lane == i1, prefix, 0.0), axis=1, keepdims=True).astype(jnp.int32)
    r2 = jnp.sum(jnp.where(lane == i2, prefix, 0.0), axis=1, keepdims=True).astype(jnp.int32)
    carry[...] = carry[...] + jnp.sum(onehot, axis=0, keepdims=True)
    cnt_ref[...] = carry[...]

    e1 = i1 - EXPERT_LANE0
    e2 = i2 - EXPERT_LANE0
    idx_ref[...] = jnp.where(lane == 0, e1, jnp.where(lane == 1, e2, jnp.where(lane == 2, r1, r2)))
    gate_ref[...] = jnp.where(lane == 0, gate1, gate2)


def _route(logits, tm):
    n = logits.shape[0]
    kern = functools.partial(_route_kernel, tm=tm)
    return pl.pallas_call(
        kern,
        grid=(n // tm,),
        in_specs=[pl.BlockSpec((tm, LANES), lambda i: (i, 0))],
        out_specs=[
            pl.BlockSpec((tm, LANES), lambda i: (i, 0)),
            pl.BlockSpec((tm, LANES), lambda i: (i, 0)),
            pl.BlockSpec((1, LANES), lambda i: (0, 0)),
        ],
        out_shape=[
            jax.ShapeDtypeStruct((n, LANES), jnp.int32),
            jax.ShapeDtypeStruct((n, LANES), jnp.float32),
            jax.ShapeDtypeStruct((1, LANES), jnp.float32),
        ],
        scratch_shapes=[pltpu.VMEM((1, LANES), jnp.float32)],
        compiler_params=_params(),
        name="route",
    )(logits)


def _dispatch_kernel(zstart_ref, h_ref, d1_ref, d2_ref, xs_ref, zbuf, sem, *, tm):
    i = pl.program_id(0)

    @pl.when(i == 0)
    def _():
        zbuf[...] = jnp.zeros_like(zbuf)

        def zero_copy(e):
            start = pl.multiple_of(zstart_ref[e], EXPERT_BLOCK)
            return pltpu.make_async_copy(zbuf, xs_ref.at[pl.ds(start, EXPERT_BLOCK)], sem)

        for e in range(N_EXPERTS):
            zero_copy(e).start()
        for e in range(N_EXPERTS):
            zero_copy(e).wait()

    def row_copy(t, dst):
        return pltpu.make_async_copy(h_ref.at[pl.ds(t, 1)], xs_ref.at[pl.ds(dst, 1)], sem)

    def start(t, carry):
        row_copy(t, d1_ref[0, 0, t]).start()
        row_copy(t, d2_ref[0, 0, t]).start()
        return carry

    def wait(t, carry):
        row_copy(0, 0).wait()
        row_copy(0, 0).wait()
        return carry

    lax.fori_loop(0, tm, start, 0)
    lax.fori_loop(0, tm, wait, 0)


def _dispatch(h1, dest1, dest2, zstart, n_slots, tm):
    n, d = h1.shape
    nt = n // tm
    kern = functools.partial(_dispatch_kernel, tm=tm)
    smem_spec = pl.BlockSpec((1, 1, tm), lambda i, z: (i, 0, 0), memory_space=pltpu.SMEM)
    return pl.pallas_call(
        kern,
        grid_spec=pltpu.PrefetchScalarGridSpec(
            num_scalar_prefetch=1,
            grid=(nt,),
            in_specs=[pl.BlockSpec((tm, d), lambda i, z: (i, 0)), smem_spec, smem_spec],
            out_specs=pl.BlockSpec(memory_space=pl.ANY),
            scratch_shapes=[pltpu.VMEM((EXPERT_BLOCK, d), jnp.float32), pltpu.SemaphoreType.DMA(())],
        ),
        out_shape=jax.ShapeDtypeStruct((n_slots, d), jnp.float32),
        compiler_params=_params(),
        name="dispatch",
    )(zstart, h1, dest1.reshape(nt, 1, tm), dest2.reshape(nt, 1, tm))


def _experts_kernel(be_ref, nu_ref, xs_ref, wg_ref, wu_ref, wd_ref, y_ref, wg_b, wu_b, wd_b):
    b = pl.program_id(0)

    @pl.when(b < nu_ref[0])
    def _():
        @pl.when((b == 0) | (be_ref[b] != be_ref[jnp.maximum(b - 1, 0)]))
        def _():
            wg_b[...] = wg_ref[0].astype(MXU_DTYPE)
            wu_b[...] = wu_ref[0].astype(MXU_DTYPE)
            wd_b[...] = wd_ref[0].astype(MXU_DTYPE)

        xb = xs_ref[...].astype(MXU_DTYPE)
        gt = _dot(xb, wg_b[...])
        up = _dot(xb, wu_b[...])
        y_ref[...] = _dot(gt * jax.nn.sigmoid(gt) * up, wd_b[...])


def _experts(xs, block_expert, n_used, w_gate, w_up, w_down, n_blocks):
    d = xs.shape[1]
    hid = w_gate.shape[-1]

    def blk(b, be, nu):
        return jnp.minimum(b, nu[0] - 1)

    return pl.pallas_call(
        _experts_kernel,
        grid_spec=pltpu.PrefetchScalarGridSpec(
            num_scalar_prefetch=2,
            grid=(n_blocks,),
            in_specs=[
                pl.BlockSpec((EXPERT_BLOCK, d), lambda b, be, nu: (blk(b, be, nu), 0)),
                pl.BlockSpec((1, d, hid), lambda b, be, nu: (be[blk(b, be, nu)], 0, 0)),
                pl.BlockSpec((1, d, hid), lambda b, be, nu: (be[blk(b, be, nu)], 0, 0)),
                pl.BlockSpec((1, hid, d), lambda b, be, nu: (be[blk(b, be, nu)], 0, 0)),
            ],
            out_specs=pl.BlockSpec((EXPERT_BLOCK, d), lambda b, be, nu: (blk(b, be, nu), 0)),
            scratch_shapes=[
                pltpu.VMEM((d, hid), MXU_DTYPE),
                pltpu.VMEM((d, hid), MXU_DTYPE),
                pltpu.VMEM((hid, d), MXU_DTYPE),
            ],
        ),
        out_shape=jax.ShapeDtypeStruct((n_blocks * EXPERT_BLOCK, d), jnp.float32),
        compiler_params=_params(),
        name="experts",
    )(block_expert, n_used, xs, w_gate, w_up, w_down)


def _combine_kernel(h_ref, gate_ref, d1_ref, d2_ref, g_ref, b_ref, ys_ref, o_ref, ybuf, sem, *, tm, alpha):
    def row_copy(k, t, src):
        return pltpu.make_async_copy(ys_ref.at[pl.ds(src, 1)], ybuf.at[k, pl.ds(t, 1)], sem)

    def start(t, carry):
        row_copy(0, t, d1_ref[0, 0, t]).start()
        row_copy(1, t, d2_ref[0, 0, t]).start()
        return carry

    def wait(t, carry):
        row_copy(0, 0, 0).wait()
        row_copy(1, 0, 0).wait()
        return carry

    lax.fori_loop(0, tm, start, 0)
    lax.fori_loop(0, tm, wait, 0)

    gates = gate_ref[...]
    ffn = ybuf[0] * gates[:, 0:1] + ybuf[1] * gates[:, 1:2]
    o_ref[...] = _layer_norm(alpha * h_ref[...] + ffn, g_ref[...], b_ref[...])


def _combine(h1, gates, dest1, dest2, ys, g, b, tm, alpha):
    n, d = h1.shape
    nt = n // tm
    kern = functools.partial(_combine_kernel, tm=tm, alpha=alpha)
    smem_spec = pl.BlockSpec((1, 1, tm), lambda i: (i, 0, 0), memory_space=pltpu.SMEM)
    return pl.pallas_call(
        kern,
        grid=(nt,),
        in_specs=[
            pl.BlockSpec((tm, d), lambda i: (i, 0)),
            pl.BlockSpec((tm, LANES), lambda i: (i, 0)),
            smem_spec,
            smem_spec,
            _const_spec((1, d)),
            _const_spec((1, d)),
            pl.BlockSpec(memory_space=pl.ANY),
        ],
        out_specs=pl.BlockSpec((tm, d), lambda i: (i, 0)),
        out_shape=jax.ShapeDtypeStruct((n, d), jnp.float32),
        scratch_shapes=[pltpu.VMEM((2, tm, d), jnp.float32), pltpu.SemaphoreType.DMA(())],
        compiler_params=_params(),
        name="combine",
    )(h1, gates, dest1.reshape(nt, 1, tm), dest2.reshape(nt, 1, tm), g.reshape(1, d), b.reshape(1, d), ys)


def _moe_layout(idx, counts_row, n_blocks, layer):
    counts = counts_row[0, EXPERT_LANE0:EXPERT_LANE0 + N_EXPERTS].astype(jnp.int32)
    padded = (counts + EXPERT_BLOCK - 1) // EXPERT_BLOCK * EXPERT_BLOCK
    pend = jnp.cumsum(padded)
    pstart = pend - padded
    dest1 = pstart[idx[:, 0]] + idx[:, 2]
    dest2 = pstart[idx[:, 1]] + idx[:, 3]
    block_expert = jnp.minimum(
        jnp.searchsorted(pend, jnp.arange(n_blocks, dtype=jnp.int32) * EXPERT_BLOCK, side="right"),
        N_EXPERTS - 1).astype(jnp.int32) + layer * N_EXPERTS
    n_used = (pend[-1:] // EXPERT_BLOCK).astype(jnp.int32)
    zstart = jnp.maximum(pend - EXPERT_BLOCK, 0)
    return dest1, dest2, zstart, block_expert, n_used


def kernel(x, meta_tokens, ln_in_g, ln_in_b, w_in, b_in, conv_w, conv_b, conv_ln_g, conv_ln_b, pool_w, pool_b,
           pool_scale, w_out, b_out, ln_mix_g, ln_mix_b, router_group_w, router_group_b, router_expert_w,
           router_expert_b, w_gate, w_up, w_down, ln_ffn_g, ln_ffn_b):
    batch, seq, d = x.shape
    depth = w_in.shape[0]
    alpha = (2.0 * depth) ** 0.25
    n = N_META + seq
    tm = _row_tile(n)
    n_blocks = -(-(2 * n + N_EXPERTS * (EXPERT_BLOCK - 1)) // EXPERT_BLOCK)
    n_slots = n_blocks * EXPERT_BLOCK

    w_in_b = w_in.astype(MXU_DTYPE)
    w_out_b = w_out.astype(MXU_DTYPE)
    pool_w_b = pool_w.astype(MXU_DTYPE)
    hid = w_gate.shape[-1]
    w_gate_s = w_gate.reshape(depth * N_EXPERTS, d, hid)
    w_up_s = w_up.reshape(depth * N_EXPERTS, d, hid)
    w_down_s = w_down.reshape(depth * N_EXPERTS, hid, d)
    pad = LANES - N_EXPERT_GROUPS - N_EXPERTS
    w_router = jnp.pad(jnp.concatenate([router_group_w, router_expert_w], axis=-1),
                       ((0, 0), (0, 0), (0, pad))).astype(MXU_DTYPE)
    b_router = jnp.pad(jnp.concatenate([router_group_b, router_expert_b], axis=-1), ((0, 0), (0, pad)))

    outs = []
    for bi in range(batch):
        h = jnp.concatenate([meta_tokens.astype(x.dtype), x[bi]], axis=0)
        h = _input_ln(h, ln_in_g, ln_in_b, tm)
        for l in range(depth):
            y = _mix_front(h, w_in_b, l, b_in[l], conv_w[l], conv_b[l], conv_ln_g[l], conv_ln_b[l],
                           pool_w_b[l], pool_b[l], pool_scale[l], tm)
            h1, logits = _mix_back(y, h, w_out_b, l, b_out[l], ln_mix_g[l], ln_mix_b[l],
                                   w_router[l], b_router[l].reshape(1, LANES), tm, alpha)
            idx, gates, counts = _route(logits, tm)
            dest1, dest2, zstart, block_expert, n_used = _moe_layout(idx, counts, n_blocks, l)
            xs = _dispatch(h1, dest1, dest2, zstart, n_slots, tm)
            ys = _experts(xs, block_expert, n_used, w_gate_s, w_up_s, w_down_s, n_blocks)
            h = _combine(h1, gates, dest1, dest2, ys, ln_ffn_g[l], ln_ffn_b[l], tm, alpha)
        outs.append(h[N_META:])
    return jnp.stack(outs, axis=0)
```

```python
import functools

import jax
import jax.numpy as jnp
from jax import lax
from jax.experimental import pallas as pl
from jax.experimental.pallas import tpu as pltpu

N_META = 16
CONV_K = 31
POOL_WINDOWS = (2, 4, 8, 16)
N_EXPERT_GROUPS = 4
EXPERTS_PER_GROUP = 8
N_EXPERTS = N_EXPERT_GROUPS * EXPERTS_PER_GROUP
LN_EPS = 1e-5

LANES = 128
SUBLANES = 8
CONV_HALO = 32
POOL_HALO = 16
ROW_TILE_CANDIDATES = (400, 656, 328, 200, 80, 16, 8)
CONV_ROW_BLOCK = 40
EXPERT_BLOCK = 256
EXPERT_LANE0 = N_EXPERT_GROUPS
VMEM_LIMIT = 56 * 1024 * 1024

MXU_DTYPE = jnp.bfloat16


def _row_tile(n_rows):
    for t in ROW_TILE_CANDIDATES:
        if n_rows % t == 0:
            return t
    raise ValueError(f"no row tile divides {n_rows}")


def _conv_row_block(tm):
    for rb in (CONV_ROW_BLOCK, 16, 8):
        if tm % rb == 0:
            return rb
    raise ValueError(f"no conv row block divides {tm}")


def _params(n_axes=1):
    return pltpu.CompilerParams(dimension_semantics=("arbitrary",) * n_axes, vmem_limit_bytes=VMEM_LIMIT)


def _const_spec(shape):
    return pl.BlockSpec(shape, lambda i: (0,) * len(shape), pipeline_mode=pl.Buffered(1))


def _layer_spec(shape, layer):
    return pl.BlockSpec((1,) + tuple(shape[1:]), lambda i: (layer, 0, 0), pipeline_mode=pl.Buffered(1))


def _layer_norm(x, g, b):
    mu = jnp.mean(x, axis=-1, keepdims=True)
    xc = x - mu
    var = jnp.mean(xc * xc, axis=-1, keepdims=True)
    return xc * lax.rsqrt(var + LN_EPS) * g + b


def _dot(a, b):
    return jnp.dot(a.astype(MXU_DTYPE), b.astype(MXU_DTYPE), preferred_element_type=jnp.float32)


def _ln_kernel(x_ref, g_ref, b_ref, o_ref):
    o_ref[...] = _layer_norm(x_ref[...], g_ref[...], b_ref[...])


def _input_ln(h, g, b, tm):
    n, d = h.shape
    return pl.pallas_call(
        _ln_kernel,
        grid=(n // tm,),
        in_specs=[pl.BlockSpec((tm, d), lambda i: (i, 0)), _const_spec((1, d)), _const_spec((1, d))],
        out_specs=pl.BlockSpec((tm, d), lambda i: (i, 0)),
        out_shape=jax.ShapeDtypeStruct((n, d), jnp.float32),
        compiler_params=_params(),
        name="input_ln",
    )(h, g.reshape(1, d), b.reshape(1, d))


def _mix_front_kernel(h_ref, w_in_ref, b_in_ref, conv_w_ref, conv_b_ref, cg_ref, cb_ref,
                      pool_w_ref, pool_b_ref, pool_scale_ref, y_ref,
                      vbuf, ubuf, shifted, conv_out, *, tm, c_conv, c_pool):
    i = pl.program_id(0)

    @pl.when(i == 0)
    def _():
        vbuf[0:CONV_HALO, :] = jnp.zeros((CONV_HALO, c_conv), jnp.float32)
        ubuf[0:POOL_HALO, :] = jnp.zeros((POOL_HALO, c_pool), jnp.float32)

    xb = h_ref[...].astype(MXU_DTYPE)
    a = _dot(xb, w_in_ref[0, :, 0:c_conv]) + b_in_ref[:, 0:c_conv]
    gate = _dot(xb, w_in_ref[0, :, c_conv:2 * c_conv]) + b_in_ref[:, c_conv:2 * c_conv]
    vbuf[CONV_HALO:CONV_HALO + tm, :] = a * jax.nn.sigmoid(gate)
    ubuf[POOL_HALO:POOL_HALO + tm, :] = _dot(xb, w_in_ref[0, :, 2 * c_conv:]) + b_in_ref[:, 2 * c_conv:]

    rb = _conv_row_block(tm)
    n_sh = tm + CONV_HALO - SUBLANES
    for c in range(c_conv // LANES):
        cs = slice(c * LANES, (c + 1) * LANES)
        for r in range(1, SUBLANES):
            shifted[r - 1, :, :] = vbuf[r:r + n_sh, cs]

        def conv_rows(j, carry):
            r0 = pl.multiple_of(j * rb, SUBLANES)
            acc = jnp.zeros((rb, LANES), jnp.float32)
            for k in range(CONV_K):
                q, r = divmod(CONV_HALO - (CONV_K - 1) + k, SUBLANES)
                start = pl.multiple_of(r0 + q * SUBLANES, SUBLANES)
                if r == 0:
                    src = vbuf[pl.ds(start, rb), cs]
                else:
                    src = shifted[r - 1, pl.ds(start, rb), :]
                acc = acc + conv_w_ref[k:k + 1, cs] * src
            conv_out[pl.ds(r0, rb), cs] = acc + conv_b_ref[:, cs]
            return carry

        lax.fori_loop(0, tm // rb, conv_rows, 0)

    v = _layer_norm(conv_out[...], cg_ref[...], cb_ref[...])
    y_ref[:, 0:c_conv] = (v * jax.nn.sigmoid(v)).astype(y_ref.dtype)

    t_glob = i * tm + lax.broadcasted_iota(jnp.int32, (tm, 1), 0)
    cg_ch = c_pool // len(POOL_WINDOWS)
    for g, w in enumerate(POOL_WINDOWS):
        gs = slice(g * cg_ch, (g + 1) * cg_ch)
        u = ubuf[POOL_HALO:POOL_HALO + tm, gs]
        s = u
        for j in range(1, w):
            s = s + ubuf[POOL_HALO - j:POOL_HALO - j + tm, gs]
        cnt = jnp.minimum(t_glob + 1, w).astype(jnp.float32)
        d = s / cnt - u
        p = _dot(d, pool_w_ref[g]) + pool_b_ref[:, gs]
        y_ref[:, c_conv + g * cg_ch:c_conv + (g + 1) * cg_ch] = (p * pool_scale_ref[:, gs]).astype(y_ref.dtype)

    vbuf[0:CONV_HALO, :] = vbuf[tm:tm + CONV_HALO, :]
    ubuf[0:POOL_HALO, :] = ubuf[tm:tm + POOL_HALO, :]


def _mix_front(h, w_in, layer, b_in, conv_w, conv_b, cg, cb, pool_w, pool_b, pool_scale, tm):
    n, d = h.shape
    c_conv = conv_w.shape[-1]
    c_pool = pool_scale.shape[-1]
    kern = functools.partial(_mix_front_kernel, tm=tm, c_conv=c_conv, c_pool=c_pool)
    return pl.pallas_call(
        kern,
        grid=(n // tm,),
        in_specs=[
            pl.BlockSpec((tm, d), lambda i: (i, 0)),
            _layer_spec(w_in.shape, layer),
            _const_spec((1, b_in.shape[-1])),
            _const_spec((CONV_K, c_conv)),
            _const_spec((1, c_conv)),
            _const_spec((1, c_conv)),
            _const_spec((1, c_conv)),
            _const_spec(pool_w.shape),
            _const_spec((1, c_pool)),
            _const_spec((1, c_pool)),
        ],
        out_specs=pl.BlockSpec((tm, c_conv + c_pool), lambda i: (i, 0)),
        out_shape=jax.ShapeDtypeStruct((n, c_conv + c_pool), MXU_DTYPE),
        scratch_shapes=[
            pltpu.VMEM((CONV_HALO + tm, c_conv), jnp.float32),
            pltpu.VMEM((POOL_HALO + tm, c_pool), jnp.float32),
            pltpu.VMEM((SUBLANES - 1, tm + CONV_HALO - SUBLANES, LANES), jnp.float32),
            pltpu.VMEM((tm, c_conv), jnp.float32),
        ],
        compiler_params=_params(),
        name="mix_front",
    )(h, w_in, b_in.reshape(1, -1), conv_w.reshape(CONV_K, c_conv), conv_b.reshape(1, -1), cg.reshape(1, -1),
      cb.reshape(1, -1), pool_w, pool_b.reshape(1, -1), pool_scale.reshape(1, -1))


def _bf16_bits(x):
    return lax.bitcast_convert_type(x.astype(jnp.bfloat16).astype(jnp.float32), jnp.uint32)


def _pack_halves(x):
    c = x.shape[1] // 2
    return (_bf16_bits(x[:, :c]) >> 16) | _bf16_bits(x[:, c:])


def _unpack_halves(w):
    left = lax.bitcast_convert_type(w << 16, jnp.float32)
    right = lax.bitcast_convert_type(w & jnp.uint32(0xFFFF0000), jnp.float32)
    return left, right


def _mix_back_kernel(y_ref, h_ref, w_out_ref, b_out_ref, g_ref, b_ref, wr_ref, br_ref, h1_ref, hp_ref, lg_ref, *,
                     alpha):
    mix = _dot(y_ref[...], w_out_ref[0]) + b_out_ref[...]
    h1 = _layer_norm(alpha * h_ref[...] + mix, g_ref[...], b_ref[...])
    h1_ref[...] = h1
    hp_ref[...] = _pack_halves(h1)
    lg_ref[...] = _dot(h1, wr_ref[...]) + br_ref[...]


def _mix_back(y, h, w_out, layer, b_out, g, b, wr, br, tm, alpha):
    n, d = h.shape
    kern = functools.partial(_mix_back_kernel, alpha=alpha)
    return pl.pallas_call(
        kern,
        grid=(n // tm,),
        in_specs=[
            pl.BlockSpec((tm, y.shape[1]), lambda i: (i, 0)),
            pl.BlockSpec((tm, d), lambda i: (i, 0)),
            _layer_spec(w_out.shape, layer),
            _const_spec((1, d)),
            _const_spec((1, d)),
            _const_spec((1, d)),
            _const_spec(wr.shape),
            _const_spec((1, LANES)),
        ],
        out_specs=[pl.BlockSpec((tm, d), lambda i: (i, 0)), pl.BlockSpec((tm, d // 2), lambda i: (i, 0)),
                   pl.BlockSpec((tm, LANES), lambda i: (i, 0))],
        out_shape=[jax.ShapeDtypeStruct((n, d), jnp.float32), jax.ShapeDtypeStruct((n, d // 2), jnp.uint32),
                   jax.ShapeDtypeStruct((n, LANES), jnp.float32)],
        compiler_params=_params(),
        name="mix_back",
    )(y, h, w_out, b_out.reshape(1, d), g.reshape(1, d), b.reshape(1, d), wr, br)


def _route_kernel(lg_ref, idx_ref, gate_ref, cnt_ref, carry, pstart, *, tm):
    p = pl.program_id(0)
    i = pl.program_id(1)

    @pl.when((p == 0) & (i == 0))
    def _():
        carry[...] = jnp.zeros_like(carry)

    @pl.when((p == 1) & (i == 0))
    def _():
        total = carry[...]
        cnt_ref[...] = total
        n_blk = jnp.ceil(total * (1.0 / EXPERT_BLOCK))
        li = lax.broadcasted_iota(jnp.int32, (LANES, LANES), 0)
        lj = lax.broadcasted_iota(jnp.int32, (LANES, LANES), 1)
        lower = (li < lj).astype(jnp.bfloat16)
        blocks_before = jnp.dot(jnp.broadcast_to(n_blk, (SUBLANES, LANES)).astype(jnp.bfloat16), lower,
                                preferred_element_type=jnp.float32)
        pstart[...] = blocks_before[0:1, :] * float(EXPERT_BLOCK)
        carry[...] = jnp.zeros_like(carry)

    lg = lg_ref[...]
    lane = lax.broadcasted_iota(jnp.int32, (tm, LANES), 1)
    neg = jnp.float32(-jnp.inf)

    def first_lane(mask):
        return jnp.min(jnp.where(mask, lane, LANES), axis=1, keepdims=True)

    gmask = lane < N_EXPERT_GROUPS
    gl = jnp.where(gmask, lg, neg)
    gmax = jnp.max(gl, axis=1, keepdims=True)
    gsum = jnp.sum(jnp.where(gmask, jnp.exp(gl - gmax), 0.0), axis=1, keepdims=True)
    g_p = 1.0 / gsum
    g_idx = first_lane(gl == gmax)

    lo = EXPERT_LANE0 + EXPERTS_PER_GROUP * g_idx
    emask = (lane >= lo) & (lane < lo + EXPERTS_PER_GROUP)
    el = jnp.where(emask, lg, neg)
    emax = jnp.max(el, axis=1, keepdims=True)
    eexp = jnp.where(emask, jnp.exp(el - emax), 0.0)
    eprob = eexp / jnp.sum(eexp, axis=1, keepdims=True)
    p1 = jnp.max(jnp.where(emask, eprob, -1.0), axis=1, keepdims=True)
    i1 = first_lane(emask & (eprob == p1))
    mask2 = emask & (lane != i1)
    p2 = jnp.max(jnp.where(mask2, eprob, -1.0), axis=1, keepdims=True)
    i2 = first_lane(mask2 & (eprob == p2))
    onehot = ((lane == i1) | (lane == i2)).astype(jnp.float32)

    @pl.when(p == 1)
    def _():
        den = p1 + p2
        gate_ref[...] = jnp.where(lane == 0, g_p * (p1 / den), g_p * (p2 / den))
        row = lax.broadcasted_iota(jnp.int32, (tm, tm), 0)
        col = lax.broadcasted_iota(jnp.int32, (tm, tm), 1)
        earlier = (row > col).astype(jnp.bfloat16)
        slot = (jnp.dot(earlier, onehot.astype(jnp.bfloat16), preferred_element_type=jnp.float32)
                + carry[...] + pstart[...])
        s1 = jnp.sum(jnp.where(lane == i1, slot, 0.0), axis=1, keepdims=True).astype(jnp.int32)
        s2 = jnp.sum(jnp.where(lane == i2, slot, 0.0), axis=1, keepdims=True).astype(jnp.int32)
        idx_ref[...] = jnp.where(lane == 0, s1, s2)

    carry[...] = carry[...] + jnp.sum(onehot, axis=0, keepdims=True)


def _route(logits, tm):
    n = logits.shape[0]
    kern = functools.partial(_route_kernel, tm=tm)
    return pl.pallas_call(
        kern,
        grid=(2, n // tm),
        in_specs=[pl.BlockSpec((tm, LANES), lambda p, i: (i, 0))],
        out_specs=[
            pl.BlockSpec((tm, LANES), lambda p, i: (i * p, 0)),
            pl.BlockSpec((tm, LANES), lambda p, i: (i * p, 0)),
            pl.BlockSpec((1, LANES), lambda p, i: (0, 0)),
        ],
        out_shape=[
            jax.ShapeDtypeStruct((n, LANES), jnp.int32),
            jax.ShapeDtypeStruct((n, LANES), jnp.float32),
            jax.ShapeDtypeStruct((1, LANES), jnp.float32),
        ],
        scratch_shapes=[pltpu.VMEM((1, LANES), jnp.float32), pltpu.VMEM((1, LANES), jnp.float32)],
        compiler_params=_params(2),
        name="route",
    )(logits)


def _experts_kernel(be_ref, nu_ref, tok_cur_ref, tok_next_ref, hp_ref, wg_ref, wu_ref, wd_ref, y_ref,
                    xbuf, sem, wg_b, wu_b, wd_b):
    b = pl.program_id(0)
    n_used = nu_ref[0]
    half = xbuf.shape[2]

    def row_copy(tok_ref, t, slot):
        return pltpu.make_async_copy(hp_ref.at[pl.ds(tok_ref[0, 0, t], 1)], xbuf.at[slot, pl.ds(t, 1)],
                                     sem.at[slot])

    def gather(tok_ref, slot):
        def start(t, carry):
            row_copy(tok_ref, t, slot).start()
            return carry
        lax.fori_loop(0, EXPERT_BLOCK, start, 0, unroll=8)

    def drain(slot):
        def wait(t, carry):
            row_copy(tok_cur_ref, 0, slot).wait()
            return carry
        lax.fori_loop(0, EXPERT_BLOCK, wait, 0, unroll=8)

    @pl.when(b == 0)
    def _():
        gather(tok_cur_ref, 0)

    @pl.when(b < n_used)
    def _():
        slot = b % 2
        gather(tok_next_ref, 1 - slot)

        @pl.when((b == 0) | (be_ref[b] != be_ref[jnp.maximum(b - 1, 0)]))
        def _():
            wg_b[...] = wg_ref[0].astype(MXU_DTYPE)
            wu_b[...] = wu_ref[0].astype(MXU_DTYPE)
            wd_b[...] = wd_ref[0].astype(MXU_DTYPE)

        drain(slot)
        left, right = _unpack_halves(xbuf[slot])
        gt = _dot(left, wg_b[0:half, :]) + _dot(right, wg_b[half:, :])
        up = _dot(left, wu_b[0:half, :]) + _dot(right, wu_b[half:, :])
        y_ref[...] = _dot(gt * jax.nn.sigmoid(gt) * up, wd_b[...])

        @pl.when(b == n_used - 1)
        def _():
            drain(1 - slot)


def _experts(hp, slot_tok, block_expert, n_used, w_gate, w_up, w_down, n_blocks):
    half = hp.shape[1]
    d = 2 * half
    hid = w_gate.shape[-1]

    def blk(b, be, nu):
        return jnp.minimum(b, nu[0] - 1)

    def nxt(b, be, nu):
        return jnp.minimum(b + 1, nu[0] - 1)

    tok_blocks = slot_tok.reshape(n_blocks, 1, EXPERT_BLOCK)
    return pl.pallas_call(
        _experts_kernel,
        grid_spec=pltpu.PrefetchScalarGridSpec(
            num_scalar_prefetch=2,
            grid=(n_blocks,),
            in_specs=[
                pl.BlockSpec((1, 1, EXPERT_BLOCK), lambda b, be, nu: (blk(b, be, nu), 0, 0), memory_space=pltpu.SMEM),
                pl.BlockSpec((1, 1, EXPERT_BLOCK), lambda b, be, nu: (nxt(b, be, nu), 0, 0), memory_space=pltpu.SMEM),
                pl.BlockSpec(memory_space=pl.ANY),
                pl.BlockSpec((1, d, hid), lambda b, be, nu: (be[blk(b, be, nu)], 0, 0)),
                pl.BlockSpec((1, d, hid), lambda b, be, nu: (be[blk(b, be, nu)], 0, 0)),
                pl.BlockSpec((1, hid, d), lambda b, be, nu: (be[blk(b, be, nu)], 0, 0)),
            ],
            out_specs=pl.BlockSpec((EXPERT_BLOCK, d), lambda b, be, nu: (blk(b, be, nu), 0)),
            scratch_shapes=[
                pltpu.VMEM((2, EXPERT_BLOCK, half), jnp.uint32),
                pltpu.SemaphoreType.DMA((2,)),
                pltpu.VMEM((d, hid), MXU_DTYPE),
                pltpu.VMEM((d, hid), MXU_DTYPE),
                pltpu.VMEM((hid, d), MXU_DTYPE),
            ],
        ),
        out_shape=jax.ShapeDtypeStruct((n_blocks * EXPERT_BLOCK, d), jnp.float32),
        compiler_params=_params(),
        name="experts",
    )(block_expert, n_used, tok_blocks, tok_blocks, hp, w_gate, w_up, w_down)


def _combine_kernel(h_ref, gate_ref, d1_ref, d2_ref, g_ref, b_ref, ys_ref, o_ref, ybuf, sem, *, tm, alpha):
    def row_copy(k, t, src):
        return pltpu.make_async_copy(ys_ref.at[pl.ds(src, 1)], ybuf.at[k, pl.ds(t, 1)], sem)

    def start(t, carry):
        row_copy(0, t, d1_ref[0, 0, t]).start()
        row_copy(1, t, d2_ref[0, 0, t]).start()
        return carry

    def wait(t, carry):
        row_copy(0, 0, 0).wait()
        row_copy(1, 0, 0).wait()
        return carry

    lax.fori_loop(0, tm, start, 0)
    lax.fori_loop(0, tm, wait, 0)

    gates = gate_ref[...]
    ffn = ybuf[0] * gates[:, 0:1] + ybuf[1] * gates[:, 1:2]
    o_ref[...] = _layer_norm(alpha * h_ref[...] + ffn, g_ref[...], b_ref[...])


def _combine(h1, gates, dest1, dest2, ys, g, b, tm, alpha):
    n, d = h1.shape
    nt = n // tm
    kern = functools.partial(_combine_kernel, tm=tm, alpha=alpha)
    smem_spec = pl.BlockSpec((1, 1, tm), lambda i: (i, 0, 0), memory_space=pltpu.SMEM)
    return pl.pallas_call(
        kern,
        grid=(nt,),
        in_specs=[
            pl.BlockSpec((tm, d), lambda i: (i, 0)),
            pl.BlockSpec((tm, LANES), lambda i: (i, 0)),
            smem_spec,
            smem_spec,
            _const_spec((1, d)),
            _const_spec((1, d)),
            pl.BlockSpec(memory_space=pl.ANY),
        ],
        out_specs=pl.BlockSpec((tm, d), lambda i: (i, 0)),
        out_shape=jax.ShapeDtypeStruct((n, d), jnp.float32),
        scratch_shapes=[pltpu.VMEM((2, tm, d), jnp.float32), pltpu.SemaphoreType.DMA(())],
        compiler_params=_params(),
        name="combine",
    )(h1, gates, dest1.reshape(nt, 1, tm), dest2.reshape(nt, 1, tm), g.reshape(1, d), b.reshape(1, d), ys)


def _moe_layout(idx, counts_row, n_blocks, layer):
    counts = counts_row[0, EXPERT_LANE0:EXPERT_LANE0 + N_EXPERTS].astype(jnp.int32)
    pend = jnp.cumsum((counts + EXPERT_BLOCK - 1) // EXPERT_BLOCK)
    blocks = jnp.arange(n_blocks, dtype=jnp.int32)
    block_expert = jnp.minimum(jnp.sum(pend[None, :] <= blocks[:, None], axis=1), N_EXPERTS - 1).astype(jnp.int32)
    block_expert = block_expert + layer * N_EXPERTS
    n_used = pend[-1:].astype(jnp.int32)
    dest1 = idx[:, 0]
    dest2 = idx[:, 1]
    tok = jnp.arange(idx.shape[0], dtype=jnp.int32)
    slot_tok = jnp.zeros((n_blocks * EXPERT_BLOCK,), jnp.int32)
    slot_tok = slot_tok.at[dest1].set(tok, unique_indices=True).at[dest2].set(tok, unique_indices=True)
    return dest1, dest2, slot_tok, block_expert, n_used


def kernel(x, meta_tokens, ln_in_g, ln_in_b, w_in, b_in, conv_w, conv_b, conv_ln_g, conv_ln_b, pool_w, pool_b,
           pool_scale, w_out, b_out, ln_mix_g, ln_mix_b, router_group_w, router_group_b, router_expert_w,
           router_expert_b, w_gate, w_up, w_down, ln_ffn_g, ln_ffn_b):
    batch, seq, d = x.shape
    depth = w_in.shape[0]
    alpha = (2.0 * depth) ** 0.25
    n = N_META + seq
    tm = _row_tile(n)
    n_blocks = -(-(2 * n + N_EXPERTS * (EXPERT_BLOCK - 1)) // EXPERT_BLOCK)

    w_in_b = w_in.astype(MXU_DTYPE)
    w_out_b = w_out.astype(MXU_DTYPE)
    pool_w_b = pool_w.astype(MXU_DTYPE)
    hid = w_gate.shape[-1]
    w_gate_s = w_gate.reshape(depth * N_EXPERTS, d, hid)
    w_up_s = w_up.reshape(depth * N_EXPERTS, d, hid)
    w_down_s = w_down.reshape(depth * N_EXPERTS, hid, d)
    pad = LANES - N_EXPERT_GROUPS - N_EXPERTS
    w_router = jnp.pad(jnp.concatenate([router_group_w, router_expert_w], axis=-1),
                       ((0, 0), (0, 0), (0, pad))).astype(MXU_DTYPE)
    b_router = jnp.pad(jnp.concatenate([router_group_b, router_expert_b], axis=-1), ((0, 0), (0, pad)))

    outs = []
    for bi in range(batch):
        h = jnp.concatenate([meta_tokens.astype(x.dtype), x[bi]], axis=0)
        h = _input_ln(h, ln_in_g, ln_in_b, tm)
        for l in range(depth):
            y = _mix_front(h, w_in_b, l, b_in[l], conv_w[l], conv_b[l], conv_ln_g[l], conv_ln_b[l],
                           pool_w_b[l], pool_b[l], pool_scale[l], tm)
            h1, hp, logits = _mix_back(y, h, w_out_b, l, b_out[l], ln_mix_g[l], ln_mix_b[l],
                                       w_router[l], b_router[l].reshape(1, LANES), tm, alpha)
            idx, gates, counts = _route(logits, tm)
            dest1, dest2, slot_tok, block_expert, n_used = _moe_layout(idx, counts, n_blocks, l)
            ys = _experts(hp, slot_tok, block_expert, n_used, w_gate_s, w_up_s, w_down_s, n_blocks)
            h = _combine(h1, gates, dest1, dest2, ys, ln_ffn_g[l], ln_ffn_b[l], tm, alpha)
        outs.append(h[N_META:])
    return jnp.stack(outs, axis=0)
```

```python
import functools

import jax
import jax.numpy as jnp
from jax import lax
from jax.experimental import pallas as pl
from jax.experimental.pallas import tpu as pltpu

N_META = 16
CONV_K = 31
POOL_WINDOWS = (2, 4, 8, 16)
N_EXPERT_GROUPS = 4
EXPERTS_PER_GROUP = 8
N_EXPERTS = N_EXPERT_GROUPS * EXPERTS_PER_GROUP
LN_EPS = 1e-5

LANES = 128
SUBLANES = 8
CONV_HALO = 32
POOL_HALO = 16
ROW_TILE_CANDIDATES = (400, 656, 328, 200, 80, 16, 8)
CONV_ROW_BLOCK = 40
EXPERT_BLOCK = 256
GATHER_UNROLL = 8
EXPERT_LANE0 = N_EXPERT_GROUPS
VMEM_LIMIT = 56 * 1024 * 1024

MXU_DTYPE = jnp.bfloat16


def _row_tile(n_rows):
    for t in ROW_TILE_CANDIDATES:
        if n_rows % t == 0:
            return t
    raise ValueError(f"no row tile divides {n_rows}")


def _conv_row_block(tm):
    for rb in (CONV_ROW_BLOCK, 16, 8):
        if tm % rb == 0:
            return rb
    raise ValueError(f"no conv row block divides {tm}")


def _params(n_axes=1):
    return pltpu.CompilerParams(dimension_semantics=("arbitrary",) * n_axes, vmem_limit_bytes=VMEM_LIMIT)


def _const_spec(shape):
    return pl.BlockSpec(shape, lambda i: (0,) * len(shape), pipeline_mode=pl.Buffered(1))


def _layer_spec(shape, layer):
    return pl.BlockSpec((1,) + tuple(shape[1:]), lambda i: (layer, 0, 0), pipeline_mode=pl.Buffered(1))


def _layer_norm(x, g, b):
    mu = jnp.mean(x, axis=-1, keepdims=True)
    xc = x - mu
    var = jnp.mean(xc * xc, axis=-1, keepdims=True)
    return xc * lax.rsqrt(var + LN_EPS) * g + b


def _dot(a, b):
    return jnp.dot(a.astype(MXU_DTYPE), b.astype(MXU_DTYPE), preferred_element_type=jnp.float32)


def _ln_kernel(x_ref, g_ref, b_ref, o_ref):
    o_ref[...] = _layer_norm(x_ref[...], g_ref[...], b_ref[...])


def _input_ln(h, g, b, tm):
    n, d = h.shape
    return pl.pallas_call(
        _ln_kernel,
        grid=(n // tm,),
        in_specs=[pl.BlockSpec((tm, d), lambda i: (i, 0)), _const_spec((1, d)), _const_spec((1, d))],
        out_specs=pl.BlockSpec((tm, d), lambda i: (i, 0)),
        out_shape=jax.ShapeDtypeStruct((n, d), jnp.float32),
        compiler_params=_params(),
        name="input_ln",
    )(h, g.reshape(1, d), b.reshape(1, d))


def _gather_two_rows(ys_ref, ybuf, sem, d1_ref, d2_ref, tm):
    def row_copy(k, t, src):
        return pltpu.make_async_copy(ys_ref.at[pl.ds(src, 1)], ybuf.at[k, pl.ds(t, 1)], sem)

    def start(t, carry):
        row_copy(0, t, d1_ref[0, 0, t]).start()
        row_copy(1, t, d2_ref[0, 0, t]).start(priority=1)
        return carry

    lax.fori_loop(0, tm, start, 0, unroll=4)


def _wait_two_rows(ys_ref, ybuf, sem, tm):
    def wait(t, carry):
        pltpu.make_async_copy(ys_ref.at[pl.ds(0, 1)], ybuf.at[0, pl.ds(0, 1)], sem).wait()
        pltpu.make_async_copy(ys_ref.at[pl.ds(0, 1)], ybuf.at[1, pl.ds(0, 1)], sem).wait()
        return carry

    lax.fori_loop(0, tm, wait, 0, unroll=4)


def _mix_front_kernel(*refs, tm, c_conv, c_pool, n_tiles, alpha, fused):
    if fused:
        (h1_ref, gate_ref, d1c_ref, d2c_ref, d1n_ref, d2n_ref, ys_ref, fg_ref, fb_ref,
         w_in_ref, b_in_ref, conv_w_ref, conv_b_ref, cg_ref, cb_ref, pool_w_ref, pool_b_ref, pool_scale_ref,
         h_ref, y_ref, vbuf, ubuf, shifted, conv_out, ybuf, sem) = refs
    else:
        (h_ref, w_in_ref, b_in_ref, conv_w_ref, conv_b_ref, cg_ref, cb_ref, pool_w_ref, pool_b_ref,
         pool_scale_ref, y_ref, vbuf, ubuf, shifted, conv_out) = refs
    i = pl.program_id(0)

    @pl.when(i == 0)
    def _():
        vbuf[0:CONV_HALO, :] = jnp.zeros((CONV_HALO, c_conv), jnp.float32)
        ubuf[0:POOL_HALO, :] = jnp.zeros((POOL_HALO, c_pool), jnp.float32)

    if fused:
        @pl.when(i == 0)
        def _():
            _gather_two_rows(ys_ref, ybuf, sem, d1c_ref, d2c_ref, tm)

        _wait_two_rows(ys_ref, ybuf, sem, tm)
        gates = gate_ref[...]
        ffn = ybuf[0] * gates[:, 0:1] + ybuf[1] * gates[:, 1:2]
        h_ref[...] = _layer_norm(alpha * h1_ref[...] + ffn, fg_ref[...], fb_ref[...])

        @pl.when(i + 1 < n_tiles)
        def _():
            _gather_two_rows(ys_ref, ybuf, sem, d1n_ref, d2n_ref, tm)

    xb = h_ref[...].astype(MXU_DTYPE)
    a = _dot(xb, w_in_ref[0, :, 0:c_conv]) + b_in_ref[:, 0:c_conv]
    gate = _dot(xb, w_in_ref[0, :, c_conv:2 * c_conv]) + b_in_ref[:, c_conv:2 * c_conv]
    vbuf[CONV_HALO:CONV_HALO + tm, :] = a * jax.nn.sigmoid(gate)
    ubuf[POOL_HALO:POOL_HALO + tm, :] = _dot(xb, w_in_ref[0, :, 2 * c_conv:]) + b_in_ref[:, 2 * c_conv:]

    rb = _conv_row_block(tm)
    n_sh = tm + CONV_HALO - SUBLANES
    for c in range(c_conv // LANES):
        cs = slice(c * LANES, (c + 1) * LANES)
        for r in range(1, SUBLANES):
            shifted[r - 1, :, :] = vbuf[r:r + n_sh, cs]

        def conv_rows(j, carry):
            r0 = pl.multiple_of(j * rb, SUBLANES)
            acc = jnp.zeros((rb, LANES), jnp.float32)
            for k in range(CONV_K):
                q, r = divmod(CONV_HALO - (CONV_K - 1) + k, SUBLANES)
                start = pl.multiple_of(r0 + q * SUBLANES, SUBLANES)
                if r == 0:
                    src = vbuf[pl.ds(start, rb), cs]
                else:
                    src = shifted[r - 1, pl.ds(start, rb), :]
                acc = acc + conv_w_ref[k:k + 1, cs] * src
            conv_out[pl.ds(r0, rb), cs] = acc + conv_b_ref[:, cs]
            return carry

        lax.fori_loop(0, tm // rb, conv_rows, 0)

    v = _layer_norm(conv_out[...], cg_ref[...], cb_ref[...])
    y_ref[:, 0:c_conv] = (v * jax.nn.sigmoid(v)).astype(y_ref.dtype)

    t_glob = i * tm + lax.broadcasted_iota(jnp.int32, (tm, 1), 0)
    cg_ch = c_pool // len(POOL_WINDOWS)
    for g, w in enumerate(POOL_WINDOWS):
        gs = slice(g * cg_ch, (g + 1) * cg_ch)
        u = ubuf[POOL_HALO:POOL_HALO + tm, gs]
        s = u
        for j in range(1, w):
            s = s + ubuf[POOL_HALO - j:POOL_HALO - j + tm, gs]
        cnt = jnp.minimum(t_glob + 1, w).astype(jnp.float32)
        d = s / cnt - u
        p = _dot(d, pool_w_ref[g]) + pool_b_ref[:, gs]
        y_ref[:, c_conv + g * cg_ch:c_conv + (g + 1) * cg_ch] = (p * pool_scale_ref[:, gs]).astype(y_ref.dtype)

    vbuf[0:CONV_HALO, :] = vbuf[tm:tm + CONV_HALO, :]
    ubuf[0:POOL_HALO, :] = ubuf[tm:tm + POOL_HALO, :]


def _mix_front(h, w_in, layer, b_in, conv_w, conv_b, cg, cb, pool_w, pool_b, pool_scale, tm, alpha, moe=None):
    n, d = h.shape
    nt = n // tm
    c_conv = conv_w.shape[-1]
    c_pool = pool_scale.shape[-1]
    fused = moe is not None
    kern = functools.partial(_mix_front_kernel, tm=tm, c_conv=c_conv, c_pool=c_pool, n_tiles=nt, alpha=alpha,
                             fused=fused)
    y_spec = pl.BlockSpec((tm, c_conv + c_pool), lambda i: (i, 0))
    y_shape = jax.ShapeDtypeStruct((n, c_conv + c_pool), MXU_DTYPE)
    scratch = [
        pltpu.VMEM((CONV_HALO + tm, c_conv), jnp.float32),
        pltpu.VMEM((POOL_HALO + tm, c_pool), jnp.float32),
        pltpu.VMEM((SUBLANES - 1, tm + CONV_HALO - SUBLANES, LANES), jnp.float32),
        pltpu.VMEM((tm, c_conv), jnp.float32),
    ]
    lead_specs, lead_args = [pl.BlockSpec((tm, d), lambda i: (i, 0))], [h]
    out_specs, out_shape = y_spec, y_shape
    if fused:
        gates, dest1, dest2, ys, ln_g, ln_b = moe
        cur = pl.BlockSpec((1, 1, tm), lambda i: (i, 0, 0), memory_space=pltpu.SMEM)
        nxt = pl.BlockSpec((1, 1, tm), lambda i: (jnp.minimum(i + 1, nt - 1), 0, 0), memory_space=pltpu.SMEM)
        d1, d2 = dest1.reshape(nt, 1, tm), dest2.reshape(nt, 1, tm)
        lead_specs += [pl.BlockSpec((tm, LANES), lambda i: (i, 0)), cur, cur, nxt, nxt,
                       pl.BlockSpec(memory_space=pl.ANY), _const_spec((1, d)), _const_spec((1, d))]
        lead_args += [gates, d1, d2, d1, d2, ys, ln_g.reshape(1, d), ln_b.reshape(1, d)]
        out_specs = [pl.BlockSpec((tm, d), lambda i: (i, 0)), y_spec]
        out_shape = [jax.ShapeDtypeStruct((n, d), jnp.float32), y_shape]
        scratch += [pltpu.VMEM((2, tm, d), jnp.float32), pltpu.SemaphoreType.DMA(())]
    return pl.pallas_call(
        kern,
        grid=(nt,),
        in_specs=lead_specs + [
            _layer_spec(w_in.shape, layer),
            _const_spec((1, b_in.shape[-1])),
            _const_spec((CONV_K, c_conv)),
            _const_spec((1, c_conv)),
            _const_spec((1, c_conv)),
            _const_spec((1, c_conv)),
            _const_spec(pool_w.shape),
            _const_spec((1, c_pool)),
            _const_spec((1, c_pool)),
        ],
        out_specs=out_specs,
        out_shape=out_shape,
        scratch_shapes=scratch,
        compiler_params=_params(),
        name="mix_front",
    )(*lead_args, w_in, b_in.reshape(1, -1), conv_w.reshape(CONV_K, c_conv), conv_b.reshape(1, -1),
      cg.reshape(1, -1), cb.reshape(1, -1), pool_w, pool_b.reshape(1, -1), pool_scale.reshape(1, -1))


def _bf16_bits(x):
    return lax.bitcast_convert_type(x.astype(jnp.bfloat16).astype(jnp.float32), jnp.uint32)


def _pack_halves(x):
    c = x.shape[1] // 2
    return (_bf16_bits(x[:, :c]) >> 16) | _bf16_bits(x[:, c:])


def _unpack_halves(w):
    left = lax.bitcast_convert_type(w << 16, jnp.float32)
    right = lax.bitcast_convert_type(w & jnp.uint32(0xFFFF0000), jnp.float32)
    return left, right


def _mix_back_kernel(y_ref, h_ref, w_out_ref, b_out_ref, g_ref, b_ref, wr_ref, br_ref, h1_ref, hp_ref, lg_ref, *,
                     alpha):
    mix = _dot(y_ref[...], w_out_ref[0]) + b_out_ref[...]
    h1 = _layer_norm(alpha * h_ref[...] + mix, g_ref[...], b_ref[...])
    h1_ref[...] = h1
    hp_ref[...] = _pack_halves(h1)
    lg_ref[...] = _dot(h1, wr_ref[...]) + br_ref[...]


def _mix_back(y, h, w_out, layer, b_out, g, b, wr, br, tm, alpha):
    n, d = h.shape
    kern = functools.partial(_mix_back_kernel, alpha=alpha)
    return pl.pallas_call(
        kern,
        grid=(n // tm,),
        in_specs=[
            pl.BlockSpec((tm, y.shape[1]), lambda i: (i, 0)),
            pl.BlockSpec((tm, d), lambda i: (i, 0)),
            _layer_spec(w_out.shape, layer),
            _const_spec((1, d)),
            _const_spec((1, d)),
            _const_spec((1, d)),
            _const_spec(wr.shape),
            _const_spec((1, LANES)),
        ],
        out_specs=[pl.BlockSpec((tm, d), lambda i: (i, 0)), pl.BlockSpec((tm, d // 2), lambda i: (i, 0)),
                   pl.BlockSpec((tm, LANES), lambda i: (i, 0))],
        out_shape=[jax.ShapeDtypeStruct((n, d), jnp.float32), jax.ShapeDtypeStruct((n, d // 2), jnp.uint32),
                   jax.ShapeDtypeStruct((n, LANES), jnp.float32)],
        compiler_params=_params(),
        name="mix_back",
    )(y, h, w_out, b_out.reshape(1, d), g.reshape(1, d), b.reshape(1, d), wr, br)


def _route_kernel(lg_ref, idx_ref, gate_ref, cnt_ref, carry, pstart, *, tm):
    p = pl.program_id(0)
    i = pl.program_id(1)

    @pl.when((p == 0) & (i == 0))
    def _():
        carry[...] = jnp.zeros_like(carry)

    @pl.when((p == 1) & (i == 0))
    def _():
        total = carry[...]
        cnt_ref[...] = total
        n_blk = jnp.ceil(total * (1.0 / EXPERT_BLOCK))
        li = lax.broadcasted_iota(jnp.int32, (LANES, LANES), 0)
        lj = lax.broadcasted_iota(jnp.int32, (LANES, LANES), 1)
        lower = (li < lj).astype(jnp.bfloat16)
        blocks_before = jnp.dot(jnp.broadcast_to(n_blk, (SUBLANES, LANES)).astype(jnp.bfloat16), lower,
                                preferred_element_type=jnp.float32)
        pstart[...] = blocks_before[0:1, :] * float(EXPERT_BLOCK)
        carry[...] = jnp.zeros_like(carry)

    lg = lg_ref[...]
    lane = lax.broadcasted_iota(jnp.int32, (tm, LANES), 1)
    neg = jnp.float32(-jnp.inf)

    def first_lane(mask):
        return jnp.min(jnp.where(mask, lane, LANES), axis=1, keepdims=True)

    gmask = lane < N_EXPERT_GROUPS
    gl = jnp.where(gmask, lg, neg)
    gmax = jnp.max(gl, axis=1, keepdims=True)
    gsum = jnp.sum(jnp.where(gmask, jnp.exp(gl - gmax), 0.0), axis=1, keepdims=True)
    g_p = 1.0 / gsum
    g_idx = first_lane(gl == gmax)

    lo = EXPERT_LANE0 + EXPERTS_PER_GROUP * g_idx
    emask = (lane >= lo) & (lane < lo + EXPERTS_PER_GROUP)
    el = jnp.where(emask, lg, neg)
    emax = jnp.max(el, axis=1, keepdims=True)
    eexp = jnp.where(emask, jnp.exp(el - emax), 0.0)
    eprob = eexp / jnp.sum(eexp, axis=1, keepdims=True)
    p1 = jnp.max(jnp.where(emask, eprob, -1.0), axis=1, keepdims=True)
    i1 = first_lane(emask & (eprob == p1))
    mask2 = emask & (lane != i1)
    p2 = jnp.max(jnp.where(mask2, eprob, -1.0), axis=1, keepdims=True)
    i2 = first_lane(mask2 & (eprob == p2))
    onehot = ((lane == i1) | (lane == i2)).astype(jnp.float32)

    @pl.when(p == 1)
    def _():
        den = p1 + p2
        gate_ref[...] = jnp.where(lane == 0, g_p * (p1 / den), g_p * (p2 / den))
        row = lax.broadcasted_iota(jnp.int32, (tm, tm), 0)
        col = lax.broadcasted_iota(jnp.int32, (tm, tm), 1)
        earlier = (row > col).astype(jnp.bfloat16)
        slot = (jnp.dot(earlier, onehot.astype(jnp.bfloat16), preferred_element_type=jnp.float32)
                + carry[...] + pstart[...])
        s1 = jnp.sum(jnp.where(lane == i1, slot, 0.0), axis=1, keepdims=True).astype(jnp.int32)
        s2 = jnp.sum(jnp.where(lane == i2, slot, 0.0), axis=1, keepdims=True).astype(jnp.int32)
        idx_ref[...] = jnp.where(lane == 0, s1, s2)

    carry[...] = carry[...] + jnp.sum(onehot, axis=0, keepdims=True)


def _route(logits, tm):
    n = logits.shape[0]
    kern = functools.partial(_route_kernel, tm=tm)
    return pl.pallas_call(
        kern,
        grid=(2, n // tm),
        in_specs=[pl.BlockSpec((tm, LANES), lambda p, i: (i, 0))],
        out_specs=[
            pl.BlockSpec((tm, LANES), lambda p, i: (i * p, 0)),
            pl.BlockSpec((tm, LANES), lambda p, i: (i * p, 0)),
            pl.BlockSpec((1, LANES), lambda p, i: (0, 0)),
        ],
        out_shape=[
            jax.ShapeDtypeStruct((n, LANES), jnp.int32),
            jax.ShapeDtypeStruct((n, LANES), jnp.float32),
            jax.ShapeDtypeStruct((1, LANES), jnp.float32),
        ],
        scratch_shapes=[pltpu.VMEM((1, LANES), jnp.float32), pltpu.VMEM((1, LANES), jnp.float32)],
        compiler_params=_params(2),
        name="route",
    )(logits)


def _experts_kernel(be_ref, nu_ref, tok_cur_ref, tok_next_ref, hp_ref, wg_ref, wu_ref, wd_ref, y_ref,
                    xbuf, sem, wg_b, wu_b, wd_b):
    b = pl.program_id(0)
    n_used = nu_ref[0]
    half = xbuf.shape[2]

    def row_copy(tok_ref, t, slot):
        return pltpu.make_async_copy(hp_ref.at[pl.ds(tok_ref[0, 0, t], 1)], xbuf.at[slot, pl.ds(t, 1)],
                                     sem.at[slot])

    def gather(tok_ref, slot):
        def start(j, carry):
            for u in range(GATHER_UNROLL):
                row_copy(tok_ref, j * GATHER_UNROLL + u, slot).start(priority=u % 2)
            return carry
        lax.fori_loop(0, EXPERT_BLOCK // GATHER_UNROLL, start, 0)

    def drain(slot):
        def wait(t, carry):
            row_copy(tok_cur_ref, 0, slot).wait()
            return carry
        lax.fori_loop(0, EXPERT_BLOCK, wait, 0, unroll=8)

    @pl.when(b == 0)
    def _():
        gather(tok_cur_ref, 0)

    @pl.when(b < n_used)
    def _():
        slot = b % 2
        gather(tok_next_ref, 1 - slot)

        @pl.when((b == 0) | (be_ref[b] != be_ref[jnp.maximum(b - 1, 0)]))
        def _():
            wg_b[...] = wg_ref[0].astype(MXU_DTYPE)
            wu_b[...] = wu_ref[0].astype(MXU_DTYPE)
            wd_b[...] = wd_ref[0].astype(MXU_DTYPE)

        drain(slot)
        left, right = _unpack_halves(xbuf[slot])
        gt = _dot(left, wg_b[0:half, :]) + _dot(right, wg_b[half:, :])
        up = _dot(left, wu_b[0:half, :]) + _dot(right, wu_b[half:, :])
        y_ref[...] = _dot(gt * jax.nn.sigmoid(gt) * up, wd_b[...])

        @pl.when(b == n_used - 1)
        def _():
            drain(1 - slot)


def _experts(hp, slot_tok, block_expert, n_used, w_gate, w_up, w_down, n_blocks):
    half = hp.shape[1]
    d = 2 * half
    hid = w_gate.shape[-1]

    def blk(b, be, nu):
        return jnp.minimum(b, nu[0] - 1)

    def nxt(b, be, nu):
        return jnp.minimum(b + 1, nu[0] - 1)

    tok_blocks = slot_tok.reshape(n_blocks, 1, EXPERT_BLOCK)
    return pl.pallas_call(
        _experts_kernel,
        grid_spec=pltpu.PrefetchScalarGridSpec(
            num_scalar_prefetch=2,
            grid=(n_blocks,),
            in_specs=[
                pl.BlockSpec((1, 1, EXPERT_BLOCK), lambda b, be, nu: (blk(b, be, nu), 0, 0), memory_space=pltpu.SMEM),
                pl.BlockSpec((1, 1, EXPERT_BLOCK), lambda b, be, nu: (nxt(b, be, nu), 0, 0), memory_space=pltpu.SMEM),
                pl.BlockSpec(memory_space=pl.ANY),
                pl.BlockSpec((1, d, hid), lambda b, be, nu: (be[blk(b, be, nu)], 0, 0)),
                pl.BlockSpec((1, d, hid), lambda b, be, nu: (be[blk(b, be, nu)], 0, 0)),
                pl.BlockSpec((1, hid, d), lambda b, be, nu: (be[blk(b, be, nu)], 0, 0)),
            ],
            out_specs=pl.BlockSpec((EXPERT_BLOCK, d), lambda b, be, nu: (blk(b, be, nu), 0)),
            scratch_shapes=[
                pltpu.VMEM((2, EXPERT_BLOCK, half), jnp.uint32),
                pltpu.SemaphoreType.DMA((2,)),
                pltpu.VMEM((d, hid), MXU_DTYPE),
                pltpu.VMEM((d, hid), MXU_DTYPE),
                pltpu.VMEM((hid, d), MXU_DTYPE),
            ],
        ),
        out_shape=jax.ShapeDtypeStruct((n_blocks * EXPERT_BLOCK, d), jnp.float32),
        compiler_params=_params(),
        name="experts",
    )(block_expert, n_used, tok_blocks, tok_blocks, hp, w_gate, w_up, w_down)


def _combine_kernel(h_ref, gate_ref, d1_ref, d2_ref, g_ref, b_ref, ys_ref, o_ref, ybuf, sem, *, tm, alpha):
    _gather_two_rows(ys_ref, ybuf, sem, d1_ref, d2_ref, tm)
    _wait_two_rows(ys_ref, ybuf, sem, tm)
    gates = gate_ref[...]
    ffn = ybuf[0] * gates[:, 0:1] + ybuf[1] * gates[:, 1:2]
    o_ref[...] = _layer_norm(alpha * h_ref[...] + ffn, g_ref[...], b_ref[...])


def _combine(h1, gates, dest1, dest2, ys, g, b, tm, alpha):
    n, d = h1.shape
    nt = n // tm
    kern = functools.partial(_combine_kernel, tm=tm, alpha=alpha)
    smem_spec = pl.BlockSpec((1, 1, tm), lambda i: (i, 0, 0), memory_space=pltpu.SMEM)
    return pl.pallas_call(
        kern,
        grid=(nt,),
        in_specs=[
            pl.BlockSpec((tm, d), lambda i: (i, 0)),
            pl.BlockSpec((tm, LANES), lambda i: (i, 0)),
            smem_spec,
            smem_spec,
            _const_spec((1, d)),
            _const_spec((1, d)),
            pl.BlockSpec(memory_space=pl.ANY),
        ],
        out_specs=pl.BlockSpec((tm, d), lambda i: (i, 0)),
        out_shape=jax.ShapeDtypeStruct((n, d), jnp.float32),
        scratch_shapes=[pltpu.VMEM((2, tm, d), jnp.float32), pltpu.SemaphoreType.DMA(())],
        compiler_params=_params(),
        name="combine",
    )(h1, gates, dest1.reshape(nt, 1, tm), dest2.reshape(nt, 1, tm), g.reshape(1, d), b.reshape(1, d), ys)


def _moe_layout(idx, counts_row, n_blocks, layer):
    counts = counts_row[0, EXPERT_LANE0:EXPERT_LANE0 + N_EXPERTS].astype(jnp.int32)
    pend = jnp.cumsum((counts + EXPERT_BLOCK - 1) // EXPERT_BLOCK)
    blocks = jnp.arange(n_blocks, dtype=jnp.int32)
    block_expert = jnp.minimum(jnp.sum(pend[None, :] <= blocks[:, None], axis=1), N_EXPERTS - 1).astype(jnp.int32)
    block_expert = block_expert + layer * N_EXPERTS
    n_used = pend[-1:].astype(jnp.int32)
    dest1 = idx[:, 0]
    dest2 = idx[:, 1]
    tok = jnp.arange(idx.shape[0], dtype=jnp.int32)
    slot_tok = jnp.zeros((n_blocks * EXPERT_BLOCK,), jnp.int32)
    slot_tok = slot_tok.at[jnp.concatenate([dest1, dest2])].set(jnp.concatenate([tok, tok]), unique_indices=True)
    return dest1, dest2, slot_tok, block_expert, n_used


def kernel(x, meta_tokens, ln_in_g, ln_in_b, w_in, b_in, conv_w, conv_b, conv_ln_g, conv_ln_b, pool_w, pool_b,
           pool_scale, w_out, b_out, ln_mix_g, ln_mix_b, router_group_w, router_group_b, router_expert_w,
           router_expert_b, w_gate, w_up, w_down, ln_ffn_g, ln_ffn_b):
    batch, seq, d = x.shape
    depth = w_in.shape[0]
    alpha = (2.0 * depth) ** 0.25
    n = N_META + seq
    tm = _row_tile(n)
    n_blocks = -(-(2 * n + N_EXPERTS * (EXPERT_BLOCK - 1)) // EXPERT_BLOCK)

    w_in_b = w_in.astype(MXU_DTYPE)
    w_out_b = w_out.astype(MXU_DTYPE)
    pool_w_b = pool_w.astype(MXU_DTYPE)
    hid = w_gate.shape[-1]
    w_gate_s = w_gate.reshape(depth * N_EXPERTS, d, hid)
    w_up_s = w_up.reshape(depth * N_EXPERTS, d, hid)
    w_down_s = w_down.reshape(depth * N_EXPERTS, hid, d)
    pad = LANES - N_EXPERT_GROUPS - N_EXPERTS
    w_router = jnp.pad(jnp.concatenate([router_group_w, router_expert_w], axis=-1),
                       ((0, 0), (0, 0), (0, pad))).astype(MXU_DTYPE)
    b_router = jnp.pad(jnp.concatenate([router_group_b, router_expert_b], axis=-1), ((0, 0), (0, pad)))

    outs = []
    for bi in range(batch):
        h = jnp.concatenate([meta_tokens.astype(x.dtype), x[bi]], axis=0)
        h = _input_ln(h, ln_in_g, ln_in_b, tm)
        moe = None
        for l in range(depth):
            mix_args = (w_in_b, l, b_in[l], conv_w[l], conv_b[l], conv_ln_g[l], conv_ln_b[l],
                        pool_w_b[l], pool_b[l], pool_scale[l], tm, alpha)
            if moe is None:
                y = _mix_front(h, *mix_args)
            else:
                h, y = _mix_front(h1, *mix_args, moe=moe)
            h1, hp, logits = _mix_back(y, h, w_out_b, l, b_out[l], ln_mix_g[l], ln_mix_b[l],
                                       w_router[l], b_router[l].reshape(1, LANES), tm, alpha)
            idx, gates, counts = _route(logits, tm)
            dest1, dest2, slot_tok, block_expert, n_used = _moe_layout(idx, counts, n_blocks, l)
            ys = _experts(hp, slot_tok, block_expert, n_used, w_gate_s, w_up_s, w_down_s, n_blocks)
            moe = (gates, dest1, dest2, ys, ln_ffn_g[l], ln_ffn_b[l])
        h = _combine(h1, *moe[:4], moe[4], moe[5], tm, alpha)
        outs.append(h[N_META:])
    return jnp.stack(outs, axis=0)
```

```python
import functools

import jax
import jax.numpy as jnp
from jax import lax
from jax.experimental import pallas as pl
from jax.experimental.pallas import tpu as pltpu
from jax.experimental.pallas import tpu_sc as plsc

N_META = 16
CONV_K = 31
POOL_WINDOWS = (2, 4, 8, 16)
N_EXPERT_GROUPS = 4
EXPERTS_PER_GROUP = 8
N_EXPERTS = N_EXPERT_GROUPS * EXPERTS_PER_GROUP
LN_EPS = 1e-5

LANES = 128
SUBLANES = 8
CONV_HALO = 32
POOL_HALO = 16
ROW_TILE_CANDIDATES = (400, 656, 328, 200, 80, 16, 8)
CONV_ROW_BLOCK = 40
EXPERT_BLOCK = 256
SC_CORES = 2
SC_SUBCORES = 16
SC_WORKERS = SC_CORES * SC_SUBCORES
SC_CHUNK = 40
EXPERT_LANE0 = N_EXPERT_GROUPS
VMEM_LIMIT = 56 * 1024 * 1024

MXU_DTYPE = jnp.bfloat16


def _row_tile(n_rows):
    for t in ROW_TILE_CANDIDATES:
        if n_rows % t == 0:
            return t
    raise ValueError(f"no row tile divides {n_rows}")


def _conv_row_block(tm):
    for rb in (CONV_ROW_BLOCK, 16, 8):
        if tm % rb == 0:
            return rb
    raise ValueError(f"no conv row block divides {tm}")


def _params(n_axes=1):
    return pltpu.CompilerParams(dimension_semantics=("arbitrary",) * n_axes, vmem_limit_bytes=VMEM_LIMIT)


def _const_spec(shape):
    return pl.BlockSpec(shape, lambda i: (0,) * len(shape), pipeline_mode=pl.Buffered(1))


def _layer_spec(shape, layer):
    return pl.BlockSpec((1,) + tuple(shape[1:]), lambda i: (layer, 0, 0), pipeline_mode=pl.Buffered(1))


def _layer_norm(x, g, b):
    mu = jnp.mean(x, axis=-1, keepdims=True)
    xc = x - mu
    var = jnp.mean(xc * xc, axis=-1, keepdims=True)
    return xc * lax.rsqrt(var + LN_EPS) * g + b


def _dot(a, b):
    return jnp.dot(a.astype(MXU_DTYPE), b.astype(MXU_DTYPE), preferred_element_type=jnp.float32)


def _ln_kernel(x_ref, g_ref, b_ref, o_ref):
    o_ref[...] = _layer_norm(x_ref[...], g_ref[...], b_ref[...])


def _input_ln(h, g, b, tm):
    n, d = h.shape
    return pl.pallas_call(
        _ln_kernel,
        grid=(n // tm,),
        in_specs=[pl.BlockSpec((tm, d), lambda i: (i, 0)), _const_spec((1, d)), _const_spec((1, d))],
        out_specs=pl.BlockSpec((tm, d), lambda i: (i, 0)),
        out_shape=jax.ShapeDtypeStruct((n, d), jnp.float32),
        compiler_params=_params(),
        name="input_ln",
    )(h, g.reshape(1, d), b.reshape(1, d))


def _moe_combine(h1_ref, gate_ref, ya_ref, yb_ref, g_ref, b_ref, alpha):
    gates = gate_ref[...]
    ffn = ya_ref[0] * gates[:, 0:1] + yb_ref[0] * gates[:, 1:2]
    return _layer_norm(alpha * h1_ref[...] + ffn, g_ref[...], b_ref[...])


def _mix_front_kernel(*refs, tm, c_conv, c_pool, alpha, fused):
    if fused:
        (h1_ref, gate_ref, ya_ref, yb_ref, fg_ref, fb_ref,
         w_in_ref, b_in_ref, conv_w_ref, conv_b_ref, cg_ref, cb_ref, pool_w_ref, pool_b_ref, pool_scale_ref,
         h_ref, y_ref, vbuf, ubuf, shifted, conv_out) = refs
    else:
        (h_ref, w_in_ref, b_in_ref, conv_w_ref, conv_b_ref, cg_ref, cb_ref, pool_w_ref, pool_b_ref,
         pool_scale_ref, y_ref, vbuf, ubuf, shifted, conv_out) = refs
    i = pl.program_id(0)

    @pl.when(i == 0)
    def _():
        vbuf[0:CONV_HALO, :] = jnp.zeros((CONV_HALO, c_conv), jnp.float32)
        ubuf[0:POOL_HALO, :] = jnp.zeros((POOL_HALO, c_pool), jnp.float32)

    if fused:
        h_ref[...] = _moe_combine(h1_ref, gate_ref, ya_ref, yb_ref, fg_ref, fb_ref, alpha)

    xb = h_ref[...].astype(MXU_DTYPE)
    a = _dot(xb, w_in_ref[0, :, 0:c_conv]) + b_in_ref[:, 0:c_conv]
    gate = _dot(xb, w_in_ref[0, :, c_conv:2 * c_conv]) + b_in_ref[:, c_conv:2 * c_conv]
    vbuf[CONV_HALO:CONV_HALO + tm, :] = a * jax.nn.sigmoid(gate)
    ubuf[POOL_HALO:POOL_HALO + tm, :] = _dot(xb, w_in_ref[0, :, 2 * c_conv:]) + b_in_ref[:, 2 * c_conv:]

    rb = _conv_row_block(tm)
    n_sh = tm + CONV_HALO - SUBLANES
    for c in range(c_conv // LANES):
        cs = slice(c * LANES, (c + 1) * LANES)
        for r in range(1, SUBLANES):
            shifted[r - 1, :, :] = vbuf[r:r + n_sh, cs]

        def conv_rows(j, carry):
            r0 = pl.multiple_of(j * rb, SUBLANES)
            acc = jnp.zeros((rb, LANES), jnp.float32)
            for k in range(CONV_K):
                q, r = divmod(CONV_HALO - (CONV_K - 1) + k, SUBLANES)
                start = pl.multiple_of(r0 + q * SUBLANES, SUBLANES)
                if r == 0:
                    src = vbuf[pl.ds(start, rb), cs]
                else:
                    src = shifted[r - 1, pl.ds(start, rb), :]
                acc = acc + conv_w_ref[k:k + 1, cs] * src
            conv_out[pl.ds(r0, rb), cs] = acc + conv_b_ref[:, cs]
            return carry

        lax.fori_loop(0, tm // rb, conv_rows, 0)

    v = _layer_norm(conv_out[...], cg_ref[...], cb_ref[...])
    y_ref[:, 0:c_conv] = (v * jax.nn.sigmoid(v)).astype(y_ref.dtype)

    t_glob = i * tm + lax.broadcasted_iota(jnp.int32, (tm, 1), 0)
    cg_ch = c_pool // len(POOL_WINDOWS)
    for g, w in enumerate(POOL_WINDOWS):
        gs = slice(g * cg_ch, (g + 1) * cg_ch)
        u = ubuf[POOL_HALO:POOL_HALO + tm, gs]
        s = u
        for j in range(1, w):
            s = s + ubuf[POOL_HALO - j:POOL_HALO - j + tm, gs]
        cnt = jnp.minimum(t_glob + 1, w).astype(jnp.float32)
        d = s / cnt - u
        p = _dot(d, pool_w_ref[g]) + pool_b_ref[:, gs]
        y_ref[:, c_conv + g * cg_ch:c_conv + (g + 1) * cg_ch] = (p * pool_scale_ref[:, gs]).astype(y_ref.dtype)

    vbuf[0:CONV_HALO, :] = vbuf[tm:tm + CONV_HALO, :]
    ubuf[0:POOL_HALO, :] = ubuf[tm:tm + POOL_HALO, :]


def _mix_front(h, w_in, layer, b_in, conv_w, conv_b, cg, cb, pool_w, pool_b, pool_scale, tm, alpha, moe=None):
    n, d = h.shape
    nt = n // tm
    c_conv = conv_w.shape[-1]
    c_pool = pool_scale.shape[-1]
    fused = moe is not None
    kern = functools.partial(_mix_front_kernel, tm=tm, c_conv=c_conv, c_pool=c_pool, alpha=alpha, fused=fused)
    y_spec = pl.BlockSpec((tm, c_conv + c_pool), lambda i: (i, 0))
    y_shape = jax.ShapeDtypeStruct((n, c_conv + c_pool), MXU_DTYPE)
    scratch = [
        pltpu.VMEM((CONV_HALO + tm, c_conv), jnp.float32),
        pltpu.VMEM((POOL_HALO + tm, c_pool), jnp.float32),
        pltpu.VMEM((SUBLANES - 1, tm + CONV_HALO - SUBLANES, LANES), jnp.float32),
        pltpu.VMEM((tm, c_conv), jnp.float32),
    ]
    lead_specs, lead_args = [pl.BlockSpec((tm, d), lambda i: (i, 0))], [h]
    out_specs, out_shape = y_spec, y_shape
    if fused:
        gates, yk, ln_g, ln_b = moe
        lead_specs += [pl.BlockSpec((tm, LANES), lambda i: (i, 0)),
                       pl.BlockSpec((1, tm, d), lambda i: (0, i, 0)), pl.BlockSpec((1, tm, d), lambda i: (1, i, 0)),
                       _const_spec((1, d)), _const_spec((1, d))]
        lead_args += [gates, yk, yk, ln_g.reshape(1, d), ln_b.reshape(1, d)]
        out_specs = [pl.BlockSpec((tm, d), lambda i: (i, 0)), y_spec]
        out_shape = [jax.ShapeDtypeStruct((n, d), jnp.float32), y_shape]
    return pl.pallas_call(
        kern,
        grid=(nt,),
        in_specs=lead_specs + [
            _layer_spec(w_in.shape, layer),
            _const_spec((1, b_in.shape[-1])),
            _const_spec((CONV_K, c_conv)),
            _const_spec((1, c_conv)),
            _const_spec((1, c_conv)),
            _const_spec((1, c_conv)),
            _const_spec(pool_w.shape),
            _const_spec((1, c_pool)),
            _const_spec((1, c_pool)),
        ],
        out_specs=out_specs,
        out_shape=out_shape,
        scratch_shapes=scratch,
        compiler_params=_params(),
        name="mix_front",
    )(*lead_args, w_in, b_in.reshape(1, -1), conv_w.reshape(CONV_K, c_conv), conv_b.reshape(1, -1),
      cg.reshape(1, -1), cb.reshape(1, -1), pool_w, pool_b.reshape(1, -1), pool_scale.reshape(1, -1))


def _bf16_bits(x):
    return lax.bitcast_convert_type(x.astype(jnp.bfloat16).astype(jnp.float32), jnp.uint32)


def _pack_halves(x):
    c = x.shape[1] // 2
    return lax.bitcast_convert_type((_bf16_bits(x[:, :c]) >> 16) | _bf16_bits(x[:, c:]), jnp.float32)


def _unpack_halves(words):
    w = lax.bitcast_convert_type(words, jnp.uint32)
    left = lax.bitcast_convert_type(w << 16, jnp.float32)
    right = lax.bitcast_convert_type(w & jnp.uint32(0xFFFF0000), jnp.float32)
    return left, right


def _mix_back_kernel(y_ref, h_ref, w_out_ref, b_out_ref, g_ref, b_ref, wr_ref, br_ref, h1_ref, hp_ref, lg_ref, *,
                     alpha):
    mix = _dot(y_ref[...], w_out_ref[0]) + b_out_ref[...]
    h1 = _layer_norm(alpha * h_ref[...] + mix, g_ref[...], b_ref[...])
    h1_ref[...] = h1
    hp_ref[...] = _pack_halves(h1)
    lg_ref[...] = _dot(h1, wr_ref[...]) + br_ref[...]


def _mix_back(y, h, w_out, layer, b_out, g, b, wr, br, tm, alpha, n_pad):
    n, d = h.shape
    kern = functools.partial(_mix_back_kernel, alpha=alpha)
    return pl.pallas_call(
        kern,
        grid=(n // tm,),
        in_specs=[
            pl.BlockSpec((tm, y.shape[1]), lambda i: (i, 0)),
            pl.BlockSpec((tm, d), lambda i: (i, 0)),
            _layer_spec(w_out.shape, layer),
            _const_spec((1, d)),
            _const_spec((1, d)),
            _const_spec((1, d)),
            _const_spec(wr.shape),
            _const_spec((1, LANES)),
        ],
        out_specs=[pl.BlockSpec((tm, d), lambda i: (i, 0)), pl.BlockSpec((tm, d // 2), lambda i: (i, 0)),
                   pl.BlockSpec((tm, LANES), lambda i: (i, 0))],
        out_shape=[jax.ShapeDtypeStruct((n, d), jnp.float32), jax.ShapeDtypeStruct((n_pad, d // 2), jnp.float32),
                   jax.ShapeDtypeStruct((n, LANES), jnp.float32)],
        compiler_params=_params(),
        name="mix_back",
    )(y, h, w_out, b_out.reshape(1, d), g.reshape(1, d), b.reshape(1, d), wr, br)


def _route_kernel(lg_ref, idx_ref, gate_ref, cnt_ref, carry, pstart, *, tm):
    p = pl.program_id(0)
    i = pl.program_id(1)

    @pl.when((p == 0) & (i == 0))
    def _():
        carry[...] = jnp.zeros_like(carry)

    @pl.when((p == 1) & (i == 0))
    def _():
        total = carry[...]
        cnt_ref[...] = total
        n_blk = jnp.ceil(total * (1.0 / EXPERT_BLOCK))
        li = lax.broadcasted_iota(jnp.int32, (LANES, LANES), 0)
        lj = lax.broadcasted_iota(jnp.int32, (LANES, LANES), 1)
        lower = (li < lj).astype(jnp.bfloat16)
        blocks_before = jnp.dot(jnp.broadcast_to(n_blk, (SUBLANES, LANES)).astype(jnp.bfloat16), lower,
                                preferred_element_type=jnp.float32)
        pstart[...] = blocks_before[0:1, :] * float(EXPERT_BLOCK)
        carry[...] = jnp.zeros_like(carry)

    lg = lg_ref[...]
    lane = lax.broadcasted_iota(jnp.int32, (tm, LANES), 1)
    neg = jnp.float32(-jnp.inf)

    def first_lane(mask):
        return jnp.min(jnp.where(mask, lane, LANES), axis=1, keepdims=True)

    gmask = lane < N_EXPERT_GROUPS
    gl = jnp.where(gmask, lg, neg)
    gmax = jnp.max(gl, axis=1, keepdims=True)
    gsum = jnp.sum(jnp.where(gmask, jnp.exp(gl - gmax), 0.0), axis=1, keepdims=True)
    g_p = 1.0 / gsum
    g_idx = first_lane(gl == gmax)

    lo = EXPERT_LANE0 + EXPERTS_PER_GROUP * g_idx
    emask = (lane >= lo) & (lane < lo + EXPERTS_PER_GROUP)
    el = jnp.where(emask, lg, neg)
    emax = jnp.max(el, axis=1, keepdims=True)
    eexp = jnp.where(emask, jnp.exp(el - emax), 0.0)
    eprob = eexp / jnp.sum(eexp, axis=1, keepdims=True)
    p1 = jnp.max(jnp.where(emask, eprob, -1.0), axis=1, keepdims=True)
    i1 = first_lane(emask & (eprob == p1))
    mask2 = emask & (lane != i1)
    p2 = jnp.max(jnp.where(mask2, eprob, -1.0), axis=1, keepdims=True)
    i2 = first_lane(mask2 & (eprob == p2))
    onehot = ((lane == i1) | (lane == i2)).astype(jnp.float32)

    @pl.when(p == 1)
    def _():
        den = p1 + p2
        gate_ref[...] = jnp.where(lane == 0, g_p * (p1 / den), g_p * (p2 / den))
        row = lax.broadcasted_iota(jnp.int32, (tm, tm), 0)
        col = lax.broadcasted_iota(jnp.int32, (tm, tm), 1)
        earlier = (row > col).astype(jnp.bfloat16)
        slot = (jnp.dot(earlier, onehot.astype(jnp.bfloat16), preferred_element_type=jnp.float32)
                + carry[...] + pstart[...])
        s1 = jnp.sum(jnp.where(lane == i1, slot, 0.0), axis=1, keepdims=True).astype(jnp.int32)
        s2 = jnp.sum(jnp.where(lane == i2, slot, 0.0), axis=1, keepdims=True).astype(jnp.int32)
        idx_ref[...] = jnp.where(lane == 0, s1, s2)

    carry[...] = carry[...] + jnp.sum(onehot, axis=0, keepdims=True)


def _route(logits, tm):
    n = logits.shape[0]
    kern = functools.partial(_route_kernel, tm=tm)
    return pl.pallas_call(
        kern,
        grid=(2, n // tm),
        in_specs=[pl.BlockSpec((tm, LANES), lambda p, i: (i, 0))],
        out_specs=[
            pl.BlockSpec((tm, LANES), lambda p, i: (i * p, 0)),
            pl.BlockSpec((tm, LANES), lambda p, i: (i * p, 0)),
            pl.BlockSpec((1, LANES), lambda p, i: (0, 0)),
        ],
        out_shape=[
            jax.ShapeDtypeStruct((n, LANES), jnp.int32),
            jax.ShapeDtypeStruct((n, LANES), jnp.float32),
            jax.ShapeDtypeStruct((1, LANES), jnp.float32),
        ],
        scratch_shapes=[pltpu.VMEM((1, LANES), jnp.float32), pltpu.VMEM((1, LANES), jnp.float32)],
        compiler_params=_params(2),
        name="route",
    )(logits)


def _sc_mesh():
    return plsc.VectorSubcoreMesh(core_axis_name="c", subcore_axis_name="s")


def _sc_worker_base(rows_per_worker):
    return (lax.axis_index("s") * SC_CORES + lax.axis_index("c")) * rows_per_worker


def _sc_dispatch(hp, d1, d2, n_out):
    n_rows, width = hp.shape
    per_worker = n_rows // SC_WORKERS
    n_chunks = per_worker // SC_CHUNK

    @functools.partial(
        pl.kernel, mesh=_sc_mesh(),
        out_type=jax.ShapeDtypeStruct((n_out, width), hp.dtype),
        scratch_types=[pltpu.VMEM((SC_CHUNK,), jnp.int32), pltpu.VMEM((SC_CHUNK,), jnp.int32),
                       pltpu.VMEM((SC_CHUNK, width), hp.dtype), pltpu.SemaphoreType.DMA],
    )
    def scatter(src_hbm, d1_hbm, d2_hbm, out_hbm, i1_v, i2_v, rows_v, sem):
        base = _sc_worker_base(per_worker)

        @pl.loop(0, n_chunks)
        def _(j):
            off = pl.multiple_of(base + j * SC_CHUNK, SUBLANES)
            pltpu.sync_copy(d1_hbm.at[pl.ds(off, SC_CHUNK)], i1_v)
            pltpu.sync_copy(d2_hbm.at[pl.ds(off, SC_CHUNK)], i2_v)
            pltpu.sync_copy(src_hbm.at[pl.ds(off, SC_CHUNK)], rows_v)
            first = pltpu.async_copy(rows_v, out_hbm.at[i1_v], sem)
            second = pltpu.async_copy(rows_v, out_hbm.at[i2_v], sem)
            first.wait()
            second.wait()

    return scatter(hp, d1, d2)


def _sc_gather(table, idx):
    n_idx = idx.shape[0]
    width = table.shape[1]
    per_worker = n_idx // SC_WORKERS
    n_chunks = per_worker // SC_CHUNK

    @functools.partial(
        pl.kernel, mesh=_sc_mesh(),
        out_type=jax.ShapeDtypeStruct((n_idx, width), table.dtype),
        scratch_types=[pltpu.VMEM((SC_CHUNK,), jnp.int32), pltpu.VMEM((SC_CHUNK, width), table.dtype),
                       pltpu.SemaphoreType.DMA],
    )
    def gather(table_hbm, idx_hbm, out_hbm, idx_v, rows_v, sem):
        base = _sc_worker_base(per_worker)

        @pl.loop(0, n_chunks)
        def _(j):
            off = pl.multiple_of(base + j * SC_CHUNK, SUBLANES)
            pltpu.sync_copy(idx_hbm.at[pl.ds(off, SC_CHUNK)], idx_v)
            pltpu.async_copy(table_hbm.at[idx_v], rows_v, sem).wait()
            pltpu.sync_copy(rows_v, out_hbm.at[pl.ds(off, SC_CHUNK)])

    return gather(table, idx)


def _experts_kernel(be_ref, nu_ref, xs_ref, wg_ref, wu_ref, wd_ref, y_ref, wg_b, wu_b, wd_b):
    b = pl.program_id(0)
    half = xs_ref.shape[1]

    @pl.when(b < nu_ref[0])
    def _():
        @pl.when((b == 0) | (be_ref[b] != be_ref[jnp.maximum(b - 1, 0)]))
        def _():
            wg_b[...] = wg_ref[0].astype(MXU_DTYPE)
            wu_b[...] = wu_ref[0].astype(MXU_DTYPE)
            wd_b[...] = wd_ref[0].astype(MXU_DTYPE)

        left, right = _unpack_halves(xs_ref[...])
        gt = _dot(left, wg_b[0:half, :]) + _dot(right, wg_b[half:, :])
        up = _dot(left, wu_b[0:half, :]) + _dot(right, wu_b[half:, :])
        y_ref[...] = _dot(gt * jax.nn.sigmoid(gt) * up, wd_b[...])


def _experts(xs, block_expert, n_used, w_gate, w_up, w_down, n_blocks):
    half = xs.shape[1]
    d = 2 * half
    hid = w_gate.shape[-1]

    def blk(b, be, nu):
        return jnp.minimum(b, nu[0] - 1)

    return pl.pallas_call(
        _experts_kernel,
        grid_spec=pltpu.PrefetchScalarGridSpec(
            num_scalar_prefetch=2,
            grid=(n_blocks,),
            in_specs=[
                pl.BlockSpec((EXPERT_BLOCK, half), lambda b, be, nu: (blk(b, be, nu), 0)),
                pl.BlockSpec((1, d, hid), lambda b, be, nu: (be[blk(b, be, nu)], 0, 0)),
                pl.BlockSpec((1, d, hid), lambda b, be, nu: (be[blk(b, be, nu)], 0, 0)),
                pl.BlockSpec((1, hid, d), lambda b, be, nu: (be[blk(b, be, nu)], 0, 0)),
            ],
            out_specs=pl.BlockSpec((EXPERT_BLOCK, d), lambda b, be, nu: (blk(b, be, nu), 0)),
            scratch_shapes=[
                pltpu.VMEM((d, hid), MXU_DTYPE),
                pltpu.VMEM((d, hid), MXU_DTYPE),
                pltpu.VMEM((hid, d), MXU_DTYPE),
            ],
        ),
        out_shape=jax.ShapeDtypeStruct((n_blocks * EXPERT_BLOCK, d), jnp.float32),
        compiler_params=_params(),
        name="experts",
    )(block_expert, n_used, xs, w_gate, w_up, w_down)


def _combine_kernel(h1_ref, gate_ref, ya_ref, yb_ref, g_ref, b_ref, o_ref, *, alpha):
    o_ref[...] = _moe_combine(h1_ref, gate_ref, ya_ref, yb_ref, g_ref, b_ref, alpha)


def _combine(h1, gates, yk, g, b, tm, alpha):
    n, d = h1.shape
    kern = functools.partial(_combine_kernel, alpha=alpha)
    return pl.pallas_call(
        kern,
        grid=(n // tm,),
        in_specs=[
            pl.BlockSpec((tm, d), lambda i: (i, 0)),
            pl.BlockSpec((tm, LANES), lambda i: (i, 0)),
            pl.BlockSpec((1, tm, d), lambda i: (0, i, 0)),
            pl.BlockSpec((1, tm, d), lambda i: (1, i, 0)),
            _const_spec((1, d)),
            _const_spec((1, d)),
        ],
        out_specs=pl.BlockSpec((tm, d), lambda i: (i, 0)),
        out_shape=jax.ShapeDtypeStruct((n, d), jnp.float32),
        compiler_params=_params(),
        name="combine",
    )(h1, gates, yk, yk, g.reshape(1, d), b.reshape(1, d))


def _moe_layout(idx, counts_row, n_blocks, layer, n_pad):
    counts = counts_row[0, EXPERT_LANE0:EXPERT_LANE0 + N_EXPERTS].astype(jnp.int32)
    pend = jnp.cumsum((counts + EXPERT_BLOCK - 1) // EXPERT_BLOCK)
    blocks = jnp.arange(n_blocks, dtype=jnp.int32)
    block_expert = jnp.minimum(jnp.sum(pend[None, :] <= blocks[:, None], axis=1), N_EXPERTS - 1).astype(jnp.int32)
    block_expert = block_expert + layer * N_EXPERTS
    n_used = pend[-1:].astype(jnp.int32)
    extra = n_pad - idx.shape[0]
    dump = jnp.full((extra,), n_blocks * EXPERT_BLOCK, jnp.int32)
    zero = jnp.zeros((extra,), jnp.int32)
    dest1, dest2 = idx[:, 0], idx[:, 1]
    scatter_idx = (jnp.concatenate([dest1, dump]), jnp.concatenate([dest2, dump]))
    gather_idx = jnp.concatenate([dest1, zero, dest2, zero])
    return scatter_idx, gather_idx, block_expert, n_used


def kernel(x, meta_tokens, ln_in_g, ln_in_b, w_in, b_in, conv_w, conv_b, conv_ln_g, conv_ln_b, pool_w, pool_b,
           pool_scale, w_out, b_out, ln_mix_g, ln_mix_b, router_group_w, router_group_b, router_expert_w,
           router_expert_b, w_gate, w_up, w_down, ln_ffn_g, ln_ffn_b):
    batch, seq, d = x.shape
    depth = w_in.shape[0]
    alpha = (2.0 * depth) ** 0.25
    n = N_META + seq
    tm = _row_tile(n)
    n_blocks = -(-(2 * n + N_EXPERTS * (EXPERT_BLOCK - 1)) // EXPERT_BLOCK)
    sc_rows = SC_WORKERS * SC_CHUNK
    n_pad = -(-n // sc_rows) * sc_rows

    w_in_b = w_in.astype(MXU_DTYPE)
    w_out_b = w_out.astype(MXU_DTYPE)
    pool_w_b = pool_w.astype(MXU_DTYPE)
    hid = w_gate.shape[-1]
    w_gate_s = w_gate.reshape(depth * N_EXPERTS, d, hid)
    w_up_s = w_up.reshape(depth * N_EXPERTS, d, hid)
    w_down_s = w_down.reshape(depth * N_EXPERTS, hid, d)
    pad = LANES - N_EXPERT_GROUPS - N_EXPERTS
    w_router = jnp.pad(jnp.concatenate([router_group_w, router_expert_w], axis=-1),
                       ((0, 0), (0, 0), (0, pad))).astype(MXU_DTYPE)
    b_router = jnp.pad(jnp.concatenate([router_group_b, router_expert_b], axis=-1), ((0, 0), (0, pad)))

    outs = []
    for bi in range(batch):
        h = jnp.concatenate([meta_tokens.astype(x.dtype), x[bi]], axis=0)
        h = _input_ln(h, ln_in_g, ln_in_b, tm)
        moe = None
        for l in range(depth):
            mix_args = (w_in_b, l, b_in[l], conv_w[l], conv_b[l], conv_ln_g[l], conv_ln_b[l],
                        pool_w_b[l], pool_b[l], pool_scale[l], tm, alpha)
            if moe is None:
                y = _mix_front(h, *mix_args)
            else:
                h, y = _mix_front(h1, *mix_args, moe=moe)
            h1, hp, logits = _mix_back(y, h, w_out_b, l, b_out[l], ln_mix_g[l], ln_mix_b[l],
                                       w_router[l], b_router[l].reshape(1, LANES), tm, alpha, n_pad)
            idx, gates, counts = _route(logits, tm)
            scatter_idx, gather_idx, block_expert, n_used = _moe_layout(idx, counts, n_blocks, l, n_pad)
            xs = _sc_dispatch(hp, *scatter_idx, n_blocks * EXPERT_BLOCK + SUBLANES)
            ys = _experts(xs, block_expert, n_used, w_gate_s, w_up_s, w_down_s, n_blocks)
            yk = _sc_gather(ys, gather_idx).reshape(2, n_pad, d)
            moe = (gates, yk, ln_ffn_g[l], ln_ffn_b[l])
        h = _combine(h1, *moe, tm, alpha)
        outs.append(h[N_META:])
    return jnp.stack(outs, axis=0)
```
